```python
import math
import jax, jax.numpy as jnp
from jax import lax
import numpy as np

D_MODEL = 2048
BATCH = 2
SEQ = 8192
DEPTH = 4

ALPHA = (2.0 * DEPTH) ** 0.25
BETA = (8.0 * DEPTH) ** -0.25
LN_EPS = 1e-5
RMS_EPS = 1e-5

SSD_D_INNER = 2 * D_MODEL
SSD_HEAD_DIM = 64
SSD_N_HEADS = SSD_D_INNER // SSD_HEAD_DIM
SSD_N_GROUPS = 8
SSD_HEADS_PER_GROUP = SSD_N_HEADS // SSD_N_GROUPS
SSD_D_STATE = 128
SSD_CONV_W = 4
SSD_CONV_DIM = SSD_D_INNER + 2 * SSD_N_GROUPS * SSD_D_STATE
SSD_CHUNK = 128

GLA_N_HEADS = 4
GLA_D_KEY = D_MODEL // 2
GLA_D_VALUE = D_MODEL
GLA_HEAD_K = GLA_D_KEY // GLA_N_HEADS
GLA_HEAD_V = GLA_D_VALUE // GLA_N_HEADS
GLA_GATE_RANK = 16
GLA_GATE_NORMALIZER = 16.0
GLA_CHUNK = 64

D_FF = 4 * D_MODEL

IN_SPLIT_SIZES = (SSD_D_INNER, SSD_CONV_DIM, SSD_N_HEADS, GLA_D_KEY, GLA_D_KEY, GLA_D_VALUE, GLA_D_VALUE, GLA_GATE_RANK, D_MODEL, D_MODEL)
D_IN_PROJ = SSD_D_INNER + SSD_CONV_DIM + SSD_N_HEADS + 2 * GLA_D_KEY + 2 * GLA_D_VALUE + GLA_GATE_RANK + 2 * D_MODEL

kernel_name = 'hybrid_ssd_gla_deepnorm'


def _split_cols(a, sizes):
    idx = np.cumsum(np.array(sizes))[:-1].tolist()
    return jnp.split(a, idx, axis=-1)


def layer_norm(x, g, b):
    xf = x.astype(jnp.float32)
    mu = jnp.mean(xf, axis=-1, keepdims=True)
    xc = xf - mu
    var = jnp.mean(xc * xc, axis=-1, keepdims=True)
    y = xc * lax.rsqrt(var + LN_EPS) * g.astype(jnp.float32) + b.astype(jnp.float32)
    return y.astype(x.dtype)


def grouped_rms_norm(y, w, n_groups):
    shp = y.shape
    yf = y.astype(jnp.float32).reshape(shp[:-1] + (n_groups, shp[-1] // n_groups))
    yf = yf * lax.rsqrt(jnp.mean(yf * yf, axis=-1, keepdims=True) + RMS_EPS)
    return (yf.reshape(shp) * w.astype(jnp.float32)).astype(y.dtype)


def causal_depthwise_conv(u, w, b):
    out = lax.conv_general_dilated(
        u, w[:, None, :].astype(u.dtype), window_strides=(1,),
        padding=[(SSD_CONV_W - 1, 0)], dimension_numbers=('NWC', 'WIO', 'NWC'),
        feature_group_count=u.shape[-1])
    return out + b.astype(u.dtype)


def ssd_chunked(xh, dt, A, Bm, Cm):
    Bsz, L = xh.shape[0], xh.shape[1]
    Q = SSD_CHUNK
    nc = L // Q
    G, Hg, P, N = SSD_N_GROUPS, SSD_HEADS_PER_GROUP, SSD_HEAD_DIM, SSD_D_STATE
    x = xh.reshape(Bsz, nc, Q, G, Hg, P)
    dtc = dt.reshape(Bsz, nc, Q, G, Hg)
    Bc = Bm.reshape(Bsz, nc, Q, G, N)
    Cc = Cm.reshape(Bsz, nc, Q, G, N)
    a_cs = jnp.cumsum(dtc * A.reshape(G, Hg), axis=2)
    xdt = x * dtc[..., None]
    mask = jnp.tril(jnp.ones((Q, Q), dtype=bool))[:, :, None, None]
    seg = a_cs[:, :, :, None] - a_cs[:, :, None, :]
    decay = jnp.exp(jnp.where(mask, seg, -jnp.inf))
    cb = jnp.einsum('bcign,bcjgn->bcijg', Cc, Bc)
    y_diag = jnp.einsum('bcijgh,bcjghp->bcighp', cb[..., None] * decay, xdt)
    a_last = a_cs[:, :, -1]
    xw = xdt * jnp.exp(a_last[:, :, None] - a_cs)[..., None]
    states = jnp.einsum('bcjgn,bcjghp->bcghpn', Bc, xw)

    def step(carry, inp):
        st, al = inp
        new = carry * jnp.exp(al)[..., None, None] + st
        return new, carry

    init = jnp.zeros((Bsz, G, Hg, P, N), dtype=states.dtype)
    _, prev = lax.scan(step, init, (jnp.moveaxis(states, 1, 0), jnp.moveaxis(a_last, 1, 0)))
    prev = jnp.moveaxis(prev, 0, 1)
    y_off = jnp.einsum('bcign,bcghpn->bcighp', Cc, prev) * jnp.exp(a_cs)[..., None]
    return (y_diag + y_off).reshape(Bsz, L, SSD_N_HEADS, P)


def gla_chunked(q, k, v, gk):
    Bsz, L = q.shape[0], q.shape[1]
    C = GLA_CHUNK
    nc = L // C
    H, K, V = GLA_N_HEADS, GLA_HEAD_K, GLA_HEAD_V
    scale = K ** -0.5
    qc = q.reshape(Bsz, nc, C, H, K)
    kc = k.reshape(Bsz, nc, C, H, K)
    vc = v.reshape(Bsz, nc, C, H, V)
    G = jnp.cumsum(gk.reshape(Bsz, nc, C, H, K), axis=2)
    G_last = G[:, :, -1:]
    qg = qc * scale * jnp.exp(G)
    kg = kc * jnp.exp(-G)
    kd = kc * jnp.exp(G_last - G)
    att = jnp.einsum('bcihk,bcjhk->bchij', qg, kg)
    att = jnp.where(jnp.tril(jnp.ones((C, C), dtype=bool)), att, 0.0)
    o_intra = jnp.einsum('bchij,bcjhv->bcihv', att, vc)

    def step(S, inp):
        q_i, k_i, v_i, d_i = inp
        o = jnp.einsum('bihk,bhkv->bihv', q_i, S)
        S = S * d_i[..., None] + jnp.einsum('bjhk,bjhv->bhkv', k_i, v_i)
        return S, o

    init = jnp.zeros((Bsz, H, K, V), dtype=q.dtype)
    xs = (jnp.moveaxis(qg, 1, 0), jnp.moveaxis(kd, 1, 0), jnp.moveaxis(vc, 1, 0),
          jnp.moveaxis(jnp.exp(G_last[:, :, 0]), 1, 0))
    _, o_inter = lax.scan(step, init, xs)
    o_inter = jnp.moveaxis(o_inter, 0, 1)
    return (o_intra + o_inter).reshape(Bsz, L, H, V)


def hybrid_mixer(h, w_in, ssd_conv_w, ssd_conv_b, ssd_dt_bias, ssd_A_log, ssd_D, ssd_norm_w,
                 gla_gk_w, gla_gk_b, gla_norm_w, w_ssd_branch, w_gla_branch, gate_bias, w_out):
    f32 = jnp.float32
    Bsz, L, _ = h.shape
    proj = h @ w_in
    z, xbc, dt_raw, q, k, v, g, gk_lr, gate_ssd, gate_gla = _split_cols(proj, IN_SPLIT_SIZES)

    xbc = jax.nn.silu(causal_depthwise_conv(xbc, ssd_conv_w, ssd_conv_b))
    xs, Bm, Cm = _split_cols(xbc, (SSD_D_INNER, SSD_N_GROUPS * SSD_D_STATE, SSD_N_GROUPS * SSD_D_STATE))
    xh = xs.reshape(Bsz, L, SSD_N_HEADS, SSD_HEAD_DIM).astype(f32)
    dt = jax.nn.softplus(dt_raw.astype(f32) + ssd_dt_bias.astype(f32))
    A = -jnp.exp(ssd_A_log.astype(f32))
    y = ssd_chunked(xh, dt, A,
                    Bm.reshape(Bsz, L, SSD_N_GROUPS, SSD_D_STATE).astype(f32),
                    Cm.reshape(Bsz, L, SSD_N_GROUPS, SSD_D_STATE).astype(f32))
    y = y + ssd_D.astype(f32)[:, None] * xh
    y = y.reshape(Bsz, L, SSD_D_INNER).astype(h.dtype)
    y = grouped_rms_norm(y * jax.nn.silu(z), ssd_norm_w, SSD_N_GROUPS)

    gk = jax.nn.log_sigmoid((gk_lr @ gla_gk_w + gla_gk_b).astype(f32)) / GLA_GATE_NORMALIZER
    o = gla_chunked(q.reshape(Bsz, L, GLA_N_HEADS, GLA_HEAD_K).astype(f32),
                    k.reshape(Bsz, L, GLA_N_HEADS, GLA_HEAD_K).astype(f32),
                    v.reshape(Bsz, L, GLA_N_HEADS, GLA_HEAD_V).astype(f32),
                    gk.reshape(Bsz, L, GLA_N_HEADS, GLA_HEAD_K))
    o = grouped_rms_norm(o.astype(h.dtype), gla_norm_w, 1) * jax.nn.silu(g).reshape(Bsz, L, GLA_N_HEADS, GLA_HEAD_V)
    o = o.reshape(Bsz, L, GLA_D_VALUE)

    branch_ssd = y @ w_ssd_branch
    branch_gla = o @ w_gla_branch
    merged = (jax.nn.sigmoid(gate_ssd + gate_bias[0]) * branch_ssd
              + jax.nn.sigmoid(gate_gla + gate_bias[1]) * branch_gla)
    return merged @ w_out


def squared_relu_mlp(h, w_up, w_down):
    return jnp.square(jax.nn.relu(h @ w_up)) @ w_down


def setup_inputs(seed: int = 0) -> dict:
    key = jax.random.key(seed)
    ks = jax.random.split(key, 21)
    nrm = jax.random.normal
    H = SSD_N_HEADS
    x = nrm(ks[0], (BATCH, SEQ, D_MODEL), jnp.float32)
    w_in = nrm(ks[1], (DEPTH, D_MODEL, D_IN_PROJ), jnp.float32) * D_MODEL ** -0.5
    ssd_conv_w = nrm(ks[2], (DEPTH, SSD_CONV_W, SSD_CONV_DIM), jnp.float32) * SSD_CONV_W ** -0.5
    ssd_conv_b = 0.02 * nrm(ks[3], (DEPTH, SSD_CONV_DIM), jnp.float32)
    dt0 = jnp.exp(jax.random.uniform(ks[4], (DEPTH, H), jnp.float32, minval=math.log(1e-3), maxval=math.log(1e-1)))
    ssd_dt_bias = dt0 + jnp.log(-jnp.expm1(-dt0))
    ssd_A_log = jnp.log(jax.random.uniform(ks[5], (DEPTH, H), jnp.float32, minval=1.0, maxval=16.0))
    ssd_D = 1.0 + 0.1 * nrm(ks[6], (DEPTH, H), jnp.float32)
    ssd_norm_w = 1.0 + 0.02 * nrm(ks[7], (DEPTH, SSD_D_INNER), jnp.float32)
    gla_gk_w = nrm(ks[8], (DEPTH, GLA_GATE_RANK, GLA_D_KEY), jnp.float32) * GLA_GATE_RANK ** -0.5
    gla_gk_b = 0.02 * nrm(ks[9], (DEPTH, GLA_D_KEY), jnp.float32)
    gla_norm_w = 1.0 + 0.02 * nrm(ks[10], (DEPTH, GLA_HEAD_V), jnp.float32)
    w_ssd_branch = nrm(ks[11], (DEPTH, SSD_D_INNER, D_MODEL), jnp.float32) * SSD_D_INNER ** -0.5
    w_gla_branch = nrm(ks[12], (DEPTH, GLA_D_VALUE, D_MODEL), jnp.float32) * GLA_D_VALUE ** -0.5
    gate_bias = 0.02 * nrm(ks[13], (DEPTH, 2, D_MODEL), jnp.float32)
    w_out = nrm(ks[14], (DEPTH, D_MODEL, D_MODEL), jnp.float32) * (D_MODEL ** -0.5 * BETA)
    ln1_g = 1.0 + 0.02 * nrm(ks[15], (DEPTH, D_MODEL), jnp.float32)
    ln1_b = 0.02 * nrm(ks[16], (DEPTH, D_MODEL), jnp.float32)
    w_up = nrm(ks[17], (DEPTH, D_MODEL, D_FF), jnp.float32) * D_MODEL ** -0.5
    w_down = nrm(ks[18], (DEPTH, D_FF, D_MODEL), jnp.float32) * (D_FF ** -0.5 * BETA)
    ln2_g = 1.0 + 0.02 * nrm(ks[19], (DEPTH, D_MODEL), jnp.float32)
    ln2_b = 0.02 * nrm(ks[20], (DEPTH, D_MODEL), jnp.float32)
    return {'x': x, 'w_in': w_in, 'ssd_conv_w': ssd_conv_w, 'ssd_conv_b': ssd_conv_b,
            'ssd_dt_bias': ssd_dt_bias, 'ssd_A_log': ssd_A_log, 'ssd_D': ssd_D, 'ssd_norm_w': ssd_norm_w,
            'gla_gk_w': gla_gk_w, 'gla_gk_b': gla_gk_b, 'gla_norm_w': gla_norm_w,
            'w_ssd_branch': w_ssd_branch, 'w_gla_branch': w_gla_branch, 'gate_bias': gate_bias,
            'w_out': w_out, 'ln1_g': ln1_g, 'ln1_b': ln1_b, 'w_up': w_up, 'w_down': w_down,
            'ln2_g': ln2_g, 'ln2_b': ln2_b}


def reference(x, w_in, ssd_conv_w, ssd_conv_b, ssd_dt_bias, ssd_A_log, ssd_D, ssd_norm_w,
              gla_gk_w, gla_gk_b, gla_norm_w, w_ssd_branch, w_gla_branch, gate_bias,
              w_out, ln1_g, ln1_b, w_up, w_down, ln2_g, ln2_b):
    for l in range(DEPTH):
        mix = hybrid_mixer(x, w_in[l], ssd_conv_w[l], ssd_conv_b[l], ssd_dt_bias[l], ssd_A_log[l],
                           ssd_D[l], ssd_norm_w[l], gla_gk_w[l], gla_gk_b[l], gla_norm_w[l],
                           w_ssd_branch[l], w_gla_branch[l], gate_bias[l], w_out[l])
        x = layer_norm(ALPHA * x + mix, ln1_g[l], ln1_b[l])
        x = layer_norm(ALPHA * x + squared_relu_mlp(x, w_up[l], w_down[l]), ln2_g[l], ln2_b[l])
    return x
```

```python
import functools

import jax
import jax.numpy as jnp
from jax import lax
from jax.experimental import pallas as pl
from jax.experimental.pallas import tpu as pltpu

F32 = jnp.float32
BF16 = jnp.bfloat16
HIGHEST = lax.Precision.HIGHEST

D_MODEL = 2048
DEPTH = 4
ALPHA = (2.0 * DEPTH) ** 0.25
LN_EPS = 1e-5
RMS_EPS = 1e-5

SSD_D_INNER = 2 * D_MODEL
SSD_HEAD_DIM = 64
SSD_N_HEADS = SSD_D_INNER // SSD_HEAD_DIM
SSD_N_GROUPS = 8
SSD_HEADS_PER_GROUP = SSD_N_HEADS // SSD_N_GROUPS
SSD_D_STATE = 128
SSD_CONV_W = 4
SSD_CHUNK = 128
SSD_GROUP_W = SSD_HEADS_PER_GROUP * SSD_HEAD_DIM
SSD_GROUP_CONV_W = SSD_GROUP_W + 2 * SSD_D_STATE

GLA_N_HEADS = 4
GLA_D_KEY = D_MODEL // 2
GLA_D_VALUE = D_MODEL
GLA_HEAD_K = GLA_D_KEY // GLA_N_HEADS
GLA_HEAD_V = GLA_D_VALUE // GLA_N_HEADS
GLA_GATE_RANK = 16
GLA_GATE_NORMALIZER = 16.0
GLA_CHUNK = 64

D_FF = 4 * D_MODEL

OFF_Z = 0
OFF_X = OFF_Z + SSD_D_INNER
OFF_B = OFF_X + SSD_D_INNER
OFF_C = OFF_B + SSD_N_GROUPS * SSD_D_STATE
OFF_Q = OFF_C + SSD_N_GROUPS * SSD_D_STATE
OFF_K = OFF_Q + GLA_D_KEY
OFF_V = OFF_K + GLA_D_KEY
OFF_G = OFF_V + GLA_D_VALUE
OFF_GS = OFF_G + GLA_D_VALUE
OFF_GG = OFF_GS + D_MODEL
N_MAIN = OFF_GG + D_MODEL
LANE = 128
SMALL_DT = 0
SMALL_GK = SSD_N_HEADS

VMEM_LIMIT = 56 * 1024 * 1024


def _cparams(n_axes):
    return pltpu.CompilerParams(dimension_semantics=("arbitrary",) * n_axes,
                                vmem_limit_bytes=VMEM_LIMIT)


def _sigmoid(x):
    return 1.0 / (1.0 + jnp.exp(-x))


def _softplus(x):
    return jnp.maximum(x, 0.0) + jnp.log(1.0 + jnp.exp(-jnp.abs(x)))


def _layer_norm(v, g, b):
    mu = jnp.mean(v, axis=-1, keepdims=True)
    vc = v - mu
    var = jnp.mean(vc * vc, axis=-1, keepdims=True)
    return vc * lax.rsqrt(var + LN_EPS) * g + b


def _mm_kernel(a_ref, b_ref, o_ref, *, act):
    acc = jnp.dot(a_ref[...], b_ref[...], preferred_element_type=F32)
    if act == "relu2":
        r = jnp.maximum(acc, 0.0)
        acc = r * r
    o_ref[...] = acc.astype(o_ref.dtype)


def _matmul(a, b, *, tm, tn, out_dtype, act=None):
    m, k = a.shape
    n = b.shape[1]
    return pl.pallas_call(
        functools.partial(_mm_kernel, act=act),
        grid=(m // tm, n // tn),
        in_specs=[pl.BlockSpec((tm, k), lambda i, j: (i, 0)),
                  pl.BlockSpec((k, tn), lambda i, j: (0, j))],
        out_specs=pl.BlockSpec((tm, tn), lambda i, j: (i, j)),
        out_shape=jax.ShapeDtypeStruct((m, n), out_dtype),
        compiler_params=_cparams(2),
    )(a, b)


def _merge_kernel(y_ref, o_ref, ws_ref, wg_ref, gs_ref, gg_ref, gb_ref, out_ref):
    bs = jnp.dot(y_ref[...], ws_ref[...], preferred_element_type=F32)
    bg = jnp.dot(o_ref[...], wg_ref[...], preferred_element_type=F32)
    gb = gb_ref[...]
    s_gate = _sigmoid(gs_ref[...].astype(F32) + gb[0:1, :])
    g_gate = _sigmoid(gg_ref[...].astype(F32) + gb[1:2, :])
    out_ref[...] = (s_gate * bs + g_gate * bg).astype(out_ref.dtype)


def _merge(y, o, proj, w_ssd, w_gla, gate_bias, *, tm=512, tn=512):
    t = y.shape[0]
    return pl.pallas_call(
        _merge_kernel,
        grid=(t // tm, D_MODEL // tn),
        in_specs=[pl.BlockSpec((tm, SSD_D_INNER), lambda i, j: (i, 0)),
                  pl.BlockSpec((tm, GLA_D_VALUE), lambda i, j: (i, 0)),
                  pl.BlockSpec((SSD_D_INNER, tn), lambda i, j: (0, j)),
                  pl.BlockSpec((GLA_D_VALUE, tn), lambda i, j: (0, j)),
                  pl.BlockSpec((tm, tn), lambda i, j: (i, OFF_GS // tn + j)),
                  pl.BlockSpec((tm, tn), lambda i, j: (i, OFF_GG // tn + j)),
                  pl.BlockSpec((2, tn), lambda i, j: (0, j))],
        out_specs=pl.BlockSpec((tm, tn), lambda i, j: (i, j)),
        out_shape=jax.ShapeDtypeStruct((t, D_MODEL), BF16),
        compiler_params=_cparams(2),
    )(y, o, w_ssd, w_gla, proj, proj, gate_bias)


def _proj_ln_kernel(a_ref, w_ref, res_ref, g_ref, b_ref, of_ref, ob_ref):
    acc = jnp.dot(a_ref[...], w_ref[...], preferred_element_type=F32)
    out = _layer_norm(ALPHA * res_ref[...] + acc, g_ref[...], b_ref[...])
    of_ref[...] = out
    ob_ref[...] = out.astype(BF16)


def _proj_ln(a, w, res, g, b, *, tm=512):
    t, k = a.shape
    return pl.pallas_call(
        _proj_ln_kernel,
        grid=(t // tm,),
        in_specs=[pl.BlockSpec((tm, k), lambda i: (i, 0)),
                  pl.BlockSpec((k, D_MODEL), lambda i: (0, 0)),
                  pl.BlockSpec((tm, D_MODEL), lambda i: (i, 0)),
                  pl.BlockSpec((1, D_MODEL), lambda i: (0, 0)),
                  pl.BlockSpec((1, D_MODEL), lambda i: (0, 0))],
        out_specs=[pl.BlockSpec((tm, D_MODEL), lambda i: (i, 0)),
                   pl.BlockSpec((tm, D_MODEL), lambda i: (i, 0))],
        out_shape=[jax.ShapeDtypeStruct((t, D_MODEL), F32),
                   jax.ShapeDtypeStruct((t, D_MODEL), BF16)],
        compiler_params=_cparams(1),
    )(a, w, res, g, b)


def _down_ln_kernel(u_ref, w_ref, res_ref, g_ref, b_ref, of_ref, ob_ref, acc_ref):
    kk = pl.program_id(1)

    @pl.when(kk == 0)
    def _():
        acc_ref[...] = jnp.zeros_like(acc_ref)

    acc_ref[...] += jnp.dot(u_ref[...], w_ref[...], preferred_element_type=F32)

    @pl.when(kk == pl.num_programs(1) - 1)
    def _():
        out = _layer_norm(ALPHA * res_ref[...] + acc_ref[...], g_ref[...], b_ref[...])
        of_ref[...] = out
        ob_ref[...] = out.astype(BF16)


def _down_ln(u, w, res, g, b, *, tm=512, tk=1024):
    t, k = u.shape
    return pl.pallas_call(
        _down_ln_kernel,
        grid=(t // tm, k // tk),
        in_specs=[pl.BlockSpec((tm, tk), lambda i, kk: (i, kk)),
                  pl.BlockSpec((tk, D_MODEL), lambda i, kk: (kk, 0)),
                  pl.BlockSpec((tm, D_MODEL), lambda i, kk: (i, 0)),
                  pl.BlockSpec((1, D_MODEL), lambda i, kk: (0, 0)),
                  pl.BlockSpec((1, D_MODEL), lambda i, kk: (0, 0))],
        out_specs=[pl.BlockSpec((tm, D_MODEL), lambda i, kk: (i, 0)),
                   pl.BlockSpec((tm, D_MODEL), lambda i, kk: (i, 0))],
        out_shape=[jax.ShapeDtypeStruct((t, D_MODEL), F32),
                   jax.ShapeDtypeStruct((t, D_MODEL), BF16)],
        scratch_shapes=[pltpu.VMEM((tm, D_MODEL), F32)],
        compiler_params=_cparams(2),
    )(u, w, res, g, b)


def _ssd_kernel(x_ref, b_ref, c_ref, z_ref, sm_ref, cw_ref, cb_ref, dtb_ref, alog_ref,
                dexp_ref, nw_ref, y_ref, xe_ref, st_ref, *, chunks_per_step):
    q = SSD_CHUNK
    hg = SSD_HEADS_PER_GROUP
    p = SSD_HEAD_DIM
    g = pl.program_id(1)

    @pl.when(pl.program_id(2) == 0)
    def _():
        xe_ref[0:8, :] = jnp.zeros((8, SSD_GROUP_CONV_W), F32)
        st_ref[...] = jnp.zeros_like(st_ref)

    row = lax.broadcasted_iota(jnp.int32, (q, q), 0)
    col = lax.broadcasted_iota(jnp.int32, (q, q), 1)
    causal = col <= row
    tri = causal.astype(F32)
    sel = jnp.logical_and(row == g * hg + col, col < hg).astype(F32)
    head_lane = col < hg

    cw = cw_ref[...]
    cbias = cb_ref[...]
    a_neg = -jnp.exp(alog_ref[...])
    dtb = dtb_ref[...]
    dexp = dexp_ref[...]
    nw = nw_ref[...]

    for cc in range(chunks_per_step):
        rows = pl.ds(cc * q, q)
        xe_ref[8:8 + q, 0:SSD_GROUP_W] = x_ref[rows, :].astype(F32)
        xe_ref[8:8 + q, SSD_GROUP_W:SSD_GROUP_W + SSD_D_STATE] = b_ref[rows, :].astype(F32)
        xe_ref[8:8 + q, SSD_GROUP_W + SSD_D_STATE:] = c_ref[rows, :].astype(F32)
        acc = cbias + cw[0:1, :] * xe_ref[5:5 + q, :]
        for k in range(1, SSD_CONV_W):
            acc = acc + cw[k:k + 1, :] * xe_ref[5 + k:5 + k + q, :]
        xe_ref[0:8, :] = xe_ref[q:q + 8, :]
        xbc = acc * _sigmoid(acc)
        xs = xbc[:, 0:SSD_GROUP_W]
        bm = xbc[:, SSD_GROUP_W:SSD_GROUP_W + SSD_D_STATE]
        cm = xbc[:, SSD_GROUP_W + SSD_D_STATE:]

        dtr = jnp.dot(sm_ref[rows, :], sel, precision=HIGHEST, preferred_element_type=F32)
        dt = _softplus(dtr + dtb)
        a = jnp.where(head_lane, dt * a_neg, 0.0)
        a_cs = jnp.dot(tri, a, precision=HIGHEST, preferred_element_type=F32)
        a_cs_t = a_cs.T[0:hg, :]
        dt_t = dt.T[0:hg, :]
        ea = jnp.exp(a_cs)
        a_last = a_cs_t[:, q - 1:q]
        wrow = dt_t * jnp.exp(a_last - a_cs_t)
        e_last = jnp.exp(a_last)

        cb = lax.dot_general(cm.astype(BF16), bm.astype(BF16), (((1,), (1,)), ((), ())),
                             preferred_element_type=F32)
        cbm = jnp.where(causal, cb, 0.0)
        bm_t = bm.T

        ys = []
        for hh in range(hg):
            lanes = slice(hh * p, (hh + 1) * p)
            seg = a_cs[:, hh:hh + 1] - a_cs_t[hh:hh + 1, :]
            w_intra = cbm * jnp.exp(jnp.minimum(seg, 0.0)) * dt_t[hh:hh + 1, :]
            c_scaled = cm * ea[:, hh:hh + 1]
            lhs = jnp.concatenate([w_intra, c_scaled], axis=1).astype(BF16)
            xh = xs[:, lanes].astype(BF16)
            prev = st_ref[:, lanes]
            rhs = jnp.concatenate([xh, prev.astype(BF16)], axis=0)
            ys.append(jnp.dot(lhs, rhs, preferred_element_type=F32))
            b_scaled = (bm_t * wrow[hh:hh + 1, :]).astype(BF16)
            st_ref[:, lanes] = prev * e_last[hh:hh + 1, :] + jnp.dot(
                b_scaled, xh, preferred_element_type=F32)

        y = jnp.concatenate(ys, axis=1) + dexp * xs
        zf = z_ref[rows, :].astype(F32)
        y = y * (zf * _sigmoid(zf))
        y = y * lax.rsqrt(jnp.mean(y * y, axis=-1, keepdims=True) + RMS_EPS) * nw
        y_ref[rows, :] = y.astype(y_ref.dtype)


def _ssd(proj, small, cw_g, cb_g, dtb_g, alog_g, dexp_g, nw_g, *, batch, seq, chunks_per_step=2):
    rows = SSD_CHUNK * chunks_per_step
    steps = seq // rows
    gw = SSD_GROUP_W
    n = SSD_D_STATE

    def row_blk(b, c):
        return b * steps + c

    return pl.pallas_call(
        functools.partial(_ssd_kernel, chunks_per_step=chunks_per_step),
        grid=(batch, SSD_N_GROUPS, steps),
        in_specs=[
            pl.BlockSpec((rows, gw), lambda b, g, c: (row_blk(b, c), OFF_X // gw + g)),
            pl.BlockSpec((rows, n), lambda b, g, c: (row_blk(b, c), OFF_B // n + g)),
            pl.BlockSpec((rows, n), lambda b, g, c: (row_blk(b, c), OFF_C // n + g)),
            pl.BlockSpec((rows, gw), lambda b, g, c: (row_blk(b, c), OFF_Z // gw + g)),
            pl.BlockSpec((rows, LANE), lambda b, g, c: (row_blk(b, c), 0)),
            pl.BlockSpec((None, SSD_CONV_W, SSD_GROUP_CONV_W), lambda b, g, c: (g, 0, 0)),
            pl.BlockSpec((None, 1, SSD_GROUP_CONV_W), lambda b, g, c: (g, 0, 0)),
            pl.BlockSpec((None, 1, LANE), lambda b, g, c: (g, 0, 0)),
            pl.BlockSpec((None, 1, LANE), lambda b, g, c: (g, 0, 0)),
            pl.BlockSpec((None, 1, gw), lambda b, g, c: (g, 0, 0)),
            pl.BlockSpec((None, 1, gw), lambda b, g, c: (g, 0, 0)),
        ],
        out_specs=pl.BlockSpec((rows, gw), lambda b, g, c: (row_blk(b, c), g)),
        out_shape=jax.ShapeDtypeStruct((batch * seq, SSD_D_INNER), BF16),
        scratch_shapes=[pltpu.VMEM((8 + SSD_CHUNK, SSD_GROUP_CONV_W), F32),
                        pltpu.VMEM((SSD_D_STATE, gw), F32)],
        compiler_params=_cparams(3),
    )(proj, proj, proj, proj, small, cw_g, cb_g, dtb_g, alog_g, dexp_g, nw_g)


def _gla_kernel(q_ref, k_ref, v_ref, g_ref, sm_ref, gkw_ref, gkb_ref, nw_ref, o_ref, st_ref,
                *, chunks_per_step):
    c = GLA_CHUNK
    scale = GLA_HEAD_K ** -0.5

    @pl.when(pl.program_id(2) == 0)
    def _():
        st_ref[...] = jnp.zeros_like(st_ref)

    row = lax.broadcasted_iota(jnp.int32, (c, c), 0)
    col = lax.broadcasted_iota(jnp.int32, (c, c), 1)
    causal = col <= row
    tri = causal.astype(F32)
    gkw = gkw_ref[...]
    gkb = gkb_ref[...]
    nw = nw_ref[...]

    for cc in range(chunks_per_step):
        rows = pl.ds(cc * c, c)
        pre = jnp.dot(sm_ref[rows, :], gkw, precision=HIGHEST, preferred_element_type=F32) + gkb
        gk = -_softplus(-pre) * (1.0 / GLA_GATE_NORMALIZER)
        gcs = jnp.dot(tri, gk, precision=HIGHEST, preferred_element_type=F32)
        g_last = gcs[c - 1:c, :]
        qf = q_ref[rows, :].astype(F32)
        kf = k_ref[rows, :].astype(F32)
        vb = v_ref[rows, :]
        qg = (qf * scale * jnp.exp(gcs)).astype(BF16)
        kg = (kf * jnp.exp(-gcs)).astype(BF16)
        kd = (kf * jnp.exp(g_last - gcs)).astype(BF16)
        att = lax.dot_general(qg, kg, (((1,), (1,)), ((), ())), preferred_element_type=F32)
        att = jnp.where(causal, att, 0.0).astype(BF16)
        st = st_ref[...]
        o = jnp.dot(att, vb, preferred_element_type=F32)
        o = o + lax.dot_general(qg, st.astype(BF16), (((1,), (1,)), ((), ())),
                                preferred_element_type=F32)
        st_ref[...] = st * jnp.exp(g_last) + lax.dot_general(
            vb, kd, (((0,), (0,)), ((), ())), preferred_element_type=F32)
        o = o * lax.rsqrt(jnp.mean(o * o, axis=-1, keepdims=True) + RMS_EPS) * nw
        gf = g_ref[rows, :].astype(F32)
        o_ref[rows, :] = (o * (gf * _sigmoid(gf))).astype(o_ref.dtype)


def _gla(proj, small, gkw_pad, gkb, nw, *, batch, seq, chunks_per_step=4):
    rows = GLA_CHUNK * chunks_per_step
    steps = seq // rows
    hk, hv = GLA_HEAD_K, GLA_HEAD_V

    def row_blk(b, c):
        return b * steps + c

    return pl.pallas_call(
        functools.partial(_gla_kernel, chunks_per_step=chunks_per_step),
        grid=(batch, GLA_N_HEADS, steps),
        in_specs=[
            pl.BlockSpec((rows, hk), lambda b, h, c: (row_blk(b, c), OFF_Q // hk + h)),
            pl.BlockSpec((rows, hk), lambda b, h, c: (row_blk(b, c), OFF_K // hk + h)),
            pl.BlockSpec((rows, hv), lambda b, h, c: (row_blk(b, c), OFF_V // hv + h)),
            pl.BlockSpec((rows, hv), lambda b, h, c: (row_blk(b, c), OFF_G // hv + h)),
            pl.BlockSpec((rows, LANE), lambda b, h, c: (row_blk(b, c), 0)),
            pl.BlockSpec((LANE, hk), lambda b, h, c: (0, h)),
            pl.BlockSpec((1, hk), lambda b, h, c: (0, h)),
            pl.BlockSpec((1, hv), lambda b, h, c: (0, 0)),
        ],
        out_specs=pl.BlockSpec((rows, hv), lambda b, h, c: (row_blk(b, c), h)),
        out_shape=jax.ShapeDtypeStruct((batch * seq, GLA_D_VALUE), BF16),
        scratch_shapes=[pltpu.VMEM((hv, hk), F32)],
        compiler_params=_cparams(3),
    )(proj, proj, proj, proj, small, gkw_pad, gkb, nw)


def _split_in_proj(w_in):
    c0 = SSD_D_INNER + SSD_D_INNER + 2 * SSD_N_GROUPS * SSD_D_STATE
    c1 = c0 + SSD_N_HEADS
    c2 = c1 + 2 * GLA_D_KEY + 2 * GLA_D_VALUE
    c3 = c2 + GLA_GATE_RANK
    w_main = jnp.concatenate([w_in[:, :c0], w_in[:, c1:c2], w_in[:, c3:]], axis=1)
    pad = jnp.zeros((w_in.shape[0], LANE - SSD_N_HEADS - GLA_GATE_RANK), w_in.dtype)
    w_small = jnp.concatenate([w_in[:, c0:c1], w_in[:, c2:c3], pad], axis=1)
    return w_main.astype(BF16), w_small.astype(BF16)


def _group_rows(v, width):
    vg = v.reshape(SSD_N_GROUPS, 1, -1)
    return jnp.pad(vg, ((0, 0), (0, 0), (0, width - vg.shape[-1])))


def _layer(xf, xb, batch, seq, w_in, conv_w, conv_b, dt_bias, a_log, d_skip, ssd_norm_w,
           gk_w, gk_b, gla_norm_w, w_ssd, w_gla, gate_bias, w_out, ln1_g, ln1_b,
           w_up, w_down, ln2_g, ln2_b):
    w_main, w_small = _split_in_proj(w_in)
    proj = _matmul(xb, w_main, tm=1024, tn=1024, out_dtype=BF16)
    small = _matmul(xb, w_small, tm=1024, tn=LANE, out_dtype=F32)

    gw, n = SSD_GROUP_W, SSD_D_STATE
    ng = SSD_N_GROUPS

    def conv_cols(v):
        lead = v.shape[:-1]
        xs = v[..., :SSD_D_INNER].reshape(lead + (ng, gw))
        bs = v[..., SSD_D_INNER:SSD_D_INNER + ng * n].reshape(lead + (ng, n))
        cs = v[..., SSD_D_INNER + ng * n:].reshape(lead + (ng, n))
        return jnp.moveaxis(jnp.concatenate([xs, bs, cs], axis=-1), -2, 0)

    cw_g = conv_cols(conv_w)
    cb_g = conv_cols(conv_b[None, :])
    dtb_g = _group_rows(dt_bias, LANE)
    alog_g = _group_rows(a_log, LANE)
    dexp_g = jnp.repeat(d_skip, SSD_HEAD_DIM).reshape(ng, 1, gw)
    nw_g = ssd_norm_w.reshape(ng, 1, gw)
    y = _ssd(proj, small, cw_g, cb_g, dtb_g, alog_g, dexp_g, nw_g, batch=batch, seq=seq)

    gkw_pad = jnp.zeros((LANE, GLA_D_KEY), F32).at[SMALL_GK:SMALL_GK + GLA_GATE_RANK].set(gk_w)
    o = _gla(proj, small, gkw_pad, gk_b[None, :], gla_norm_w[None, :], batch=batch, seq=seq)

    merged = _merge(y, o, proj, w_ssd.astype(BF16), w_gla.astype(BF16), gate_bias)
    hf, hb = _proj_ln(merged, w_out.astype(BF16), xf, ln1_g[None, :], ln1_b[None, :])
    u = _matmul(hb, w_up.astype(BF16), tm=1024, tn=1024, out_dtype=BF16, act="relu2")
    return _down_ln(u, w_down.astype(BF16), hf, ln2_g[None, :], ln2_b[None, :])


def kernel(x, w_in, ssd_conv_w, ssd_conv_b, ssd_dt_bias, ssd_A_log, ssd_D, ssd_norm_w, gla_gk_w, gla_gk_b, gla_norm_w, w_ssd_branch, w_gla_branch, gate_bias, w_out, ln1_g, ln1_b, w_up, w_down, ln2_g, ln2_b):
    batch, seq, d = x.shape
    xf = x.reshape(batch * seq, d)
    xb = xf.astype(BF16)
    for l in range(w_in.shape[0]):
        xf, xb = _layer(xf, xb, batch, seq, w_in[l], ssd_conv_w[l], ssd_conv_b[l], ssd_dt_bias[l],
                        ssd_A_log[l], ssd_D[l], ssd_norm_w[l], gla_gk_w[l], gla_gk_b[l],
                        gla_norm_w[l], w_ssd_branch[l], w_gla_branch[l], gate_bias[l], w_out[l],
                        ln1_g[l], ln1_b[l], w_up[l], w_down[l], ln2_g[l], ln2_b[l])
    return xf.reshape(batch, seq, d)
```

```python
import functools

import jax
import jax.numpy as jnp
from jax import lax
from jax.experimental import pallas as pl
from jax.experimental.pallas import tpu as pltpu

F32 = jnp.float32
BF16 = jnp.bfloat16
HIGHEST = lax.Precision.HIGHEST

D_MODEL = 2048
DEPTH = 4
ALPHA = (2.0 * DEPTH) ** 0.25
LN_EPS = 1e-5
RMS_EPS = 1e-5

SSD_D_INNER = 2 * D_MODEL
SSD_HEAD_DIM = 64
SSD_N_HEADS = SSD_D_INNER // SSD_HEAD_DIM
SSD_N_GROUPS = 8
SSD_HEADS_PER_GROUP = SSD_N_HEADS // SSD_N_GROUPS
SSD_D_STATE = 128
SSD_CONV_W = 4
SSD_CHUNK = 128
SSD_GROUP_W = SSD_HEADS_PER_GROUP * SSD_HEAD_DIM
SSD_GROUP_CONV_W = SSD_GROUP_W + 2 * SSD_D_STATE

GLA_N_HEADS = 4
GLA_D_KEY = D_MODEL // 2
GLA_D_VALUE = D_MODEL
GLA_HEAD_K = GLA_D_KEY // GLA_N_HEADS
GLA_HEAD_V = GLA_D_VALUE // GLA_N_HEADS
GLA_GATE_RANK = 16
GLA_GATE_NORMALIZER = 16.0
GLA_CHUNK = 64

D_FF = 4 * D_MODEL

OFF_Z = 0
OFF_X = OFF_Z + SSD_D_INNER
OFF_B = OFF_X + SSD_D_INNER
OFF_C = OFF_B + SSD_N_GROUPS * SSD_D_STATE
OFF_Q = OFF_C + SSD_N_GROUPS * SSD_D_STATE
OFF_K = OFF_Q + GLA_D_KEY
OFF_V = OFF_K + GLA_D_KEY
OFF_G = OFF_V + GLA_D_VALUE
OFF_GS = OFF_G + GLA_D_VALUE
OFF_GG = OFF_GS + D_MODEL
N_MAIN = OFF_GG + D_MODEL
LANE = 128
SMALL_DT = 0
SMALL_GK = SSD_N_HEADS

VMEM_LIMIT = 56 * 1024 * 1024


def _cparams(n_axes):
    return pltpu.CompilerParams(dimension_semantics=("arbitrary",) * n_axes,
                                vmem_limit_bytes=VMEM_LIMIT)


def _sigmoid(x):
    return 0.5 + 0.5 * jnp.tanh(0.5 * x)


def _silu(x):
    h = 0.5 * x
    return h + h * jnp.tanh(h)


def _softplus(x):
    return jnp.maximum(x, 0.0) + jnp.log(1.0 + jnp.exp(-jnp.abs(x)))


def _layer_norm(v, g, b):
    mu = jnp.mean(v, axis=-1, keepdims=True)
    vc = v - mu
    var = jnp.mean(vc * vc, axis=-1, keepdims=True)
    return vc * lax.rsqrt(var + LN_EPS) * g + b


def _mm_kernel(a_ref, b_ref, o_ref, *, act):
    acc = jnp.dot(a_ref[...], b_ref[...], preferred_element_type=F32)
    if act == "relu2":
        r = jnp.maximum(acc, 0.0)
        acc = r * r
    o_ref[...] = acc.astype(o_ref.dtype)


def _matmul(a, b, *, tm, tn, out_dtype, act=None):
    m, k = a.shape
    n = b.shape[1]
    return pl.pallas_call(
        functools.partial(_mm_kernel, act=act),
        grid=(m // tm, n // tn),
        in_specs=[pl.BlockSpec((tm, k), lambda i, j: (i, 0)),
                  pl.BlockSpec((k, tn), lambda i, j: (0, j))],
        out_specs=pl.BlockSpec((tm, tn), lambda i, j: (i, j)),
        out_shape=jax.ShapeDtypeStruct((m, n), out_dtype),
        compiler_params=_cparams(2),
    )(a, b)


def _small_proj_kernel(a_ref, b_ref, o_ref, ot_ref):
    acc = jnp.dot(a_ref[...], b_ref[...], preferred_element_type=F32)
    o_ref[...] = acc
    ot_ref[...] = acc.T


def _small_proj(a, b, *, tm=1024):
    m, k = a.shape
    return pl.pallas_call(
        _small_proj_kernel,
        grid=(m // tm,),
        in_specs=[pl.BlockSpec((tm, k), lambda i: (i, 0)),
                  pl.BlockSpec((k, LANE), lambda i: (0, 0))],
        out_specs=[pl.BlockSpec((tm, LANE), lambda i: (i, 0)),
                   pl.BlockSpec((LANE, tm), lambda i: (0, i))],
        out_shape=[jax.ShapeDtypeStruct((m, LANE), F32),
                   jax.ShapeDtypeStruct((LANE, m), F32)],
        compiler_params=_cparams(1),
    )(a, b)


def _merge_kernel(y_ref, o_ref, ws_ref, wg_ref, gs_ref, gg_ref, gb_ref, out_ref):
    bs = jnp.dot(y_ref[...], ws_ref[...], preferred_element_type=F32)
    bg = jnp.dot(o_ref[...], wg_ref[...], preferred_element_type=F32)
    gb = gb_ref[...]
    s_gate = _sigmoid(gs_ref[...].astype(F32) + gb[0:1, :])
    g_gate = _sigmoid(gg_ref[...].astype(F32) + gb[1:2, :])
    out_ref[...] = (s_gate * bs + g_gate * bg).astype(out_ref.dtype)


def _merge(y, o, proj, w_ssd, w_gla, gate_bias, *, tm=512, tn=512):
    t = y.shape[0]
    return pl.pallas_call(
        _merge_kernel,
        grid=(t // tm, D_MODEL // tn),
        in_specs=[pl.BlockSpec((tm, SSD_D_INNER), lambda i, j: (i, 0)),
                  pl.BlockSpec((tm, GLA_D_VALUE), lambda i, j: (i, 0)),
                  pl.BlockSpec((SSD_D_INNER, tn), lambda i, j: (0, j)),
                  pl.BlockSpec((GLA_D_VALUE, tn), lambda i, j: (0, j)),
                  pl.BlockSpec((tm, tn), lambda i, j: (i, OFF_GS // tn + j)),
                  pl.BlockSpec((tm, tn), lambda i, j: (i, OFF_GG // tn + j)),
                  pl.BlockSpec((2, tn), lambda i, j: (0, j))],
        out_specs=pl.BlockSpec((tm, tn), lambda i, j: (i, j)),
        out_shape=jax.ShapeDtypeStruct((t, D_MODEL), BF16),
        compiler_params=_cparams(2),
    )(y, o, w_ssd, w_gla, proj, proj, gate_bias)


def _proj_ln_kernel(a_ref, w_ref, res_ref, g_ref, b_ref, of_ref, ob_ref):
    acc = jnp.dot(a_ref[...], w_ref[...], preferred_element_type=F32)
    out = _layer_norm(ALPHA * res_ref[...] + acc, g_ref[...], b_ref[...])
    of_ref[...] = out
    ob_ref[...] = out.astype(BF16)


def _proj_ln(a, w, res, g, b, *, tm=512):
    t, k = a.shape
    return pl.pallas_call(
        _proj_ln_kernel,
        grid=(t // tm,),
        in_specs=[pl.BlockSpec((tm, k), lambda i: (i, 0)),
                  pl.BlockSpec((k, D_MODEL), lambda i: (0, 0)),
                  pl.BlockSpec((tm, D_MODEL), lambda i: (i, 0)),
                  pl.BlockSpec((1, D_MODEL), lambda i: (0, 0)),
                  pl.BlockSpec((1, D_MODEL), lambda i: (0, 0))],
        out_specs=[pl.BlockSpec((tm, D_MODEL), lambda i: (i, 0)),
                   pl.BlockSpec((tm, D_MODEL), lambda i: (i, 0))],
        out_shape=[jax.ShapeDtypeStruct((t, D_MODEL), F32),
                   jax.ShapeDtypeStruct((t, D_MODEL), BF16)],
        compiler_params=_cparams(1),
    )(a, w, res, g, b)


def _down_ln_kernel(u_ref, w_ref, res_ref, g_ref, b_ref, of_ref, ob_ref, acc_ref):
    kk = pl.program_id(1)

    @pl.when(kk == 0)
    def _():
        acc_ref[...] = jnp.zeros_like(acc_ref)

    acc_ref[...] += jnp.dot(u_ref[...], w_ref[...], preferred_element_type=F32)

    @pl.when(kk == pl.num_programs(1) - 1)
    def _():
        out = _layer_norm(ALPHA * res_ref[...] + acc_ref[...], g_ref[...], b_ref[...])
        of_ref[...] = out
        ob_ref[...] = out.astype(BF16)


def _down_ln(u, w, res, g, b, *, tm=512, tk=1024):
    t, k = u.shape
    return pl.pallas_call(
        _down_ln_kernel,
        grid=(t // tm, k // tk),
        in_specs=[pl.BlockSpec((tm, tk), lambda i, kk: (i, kk)),
                  pl.BlockSpec((tk, D_MODEL), lambda i, kk: (kk, 0)),
                  pl.BlockSpec((tm, D_MODEL), lambda i, kk: (i, 0)),
                  pl.BlockSpec((1, D_MODEL), lambda i, kk: (0, 0)),
                  pl.BlockSpec((1, D_MODEL), lambda i, kk: (0, 0))],
        out_specs=[pl.BlockSpec((tm, D_MODEL), lambda i, kk: (i, 0)),
                   pl.BlockSpec((tm, D_MODEL), lambda i, kk: (i, 0))],
        out_shape=[jax.ShapeDtypeStruct((t, D_MODEL), F32),
                   jax.ShapeDtypeStruct((t, D_MODEL), BF16)],
        scratch_shapes=[pltpu.VMEM((tm, D_MODEL), F32)],
        compiler_params=_cparams(2),
    )(u, w, res, g, b)


_EXP_SKIP = 1
_EXP_SECTIONS = 4


def _ssd_expand_matrix():
    r = lax.broadcasted_iota(jnp.int32, (LANE, _EXP_SECTIONS * SSD_GROUP_W), 0)
    c = lax.broadcasted_iota(jnp.int32, (LANE, _EXP_SECTIONS * SSD_GROUP_W), 1)
    src = (c // SSD_GROUP_W + _EXP_SKIP) * SSD_HEADS_PER_GROUP + (c % SSD_GROUP_W) // SSD_HEAD_DIM
    e = (r == src).astype(BF16)
    return jnp.concatenate([e, e], axis=0)


def _ssd_shift_matrix():
    q = SSD_CHUNK
    r = lax.broadcasted_iota(jnp.int32, (3 * q, 2 * q), 0)
    c = lax.broadcasted_iota(jnp.int32, (3 * q, 2 * q), 1)
    return (c == q + r % q - (r // q + 1)).astype(BF16)


def _split3(v):
    hi = v.astype(BF16)
    r1 = v - hi.astype(F32)
    mid = r1.astype(BF16)
    lo = (r1 - mid.astype(F32)).astype(BF16)
    return hi, mid, lo


def _ssd_kernel(x_ref, b_ref, c_ref, z_ref, dt_ref, cw_ref, cb_ref, dtb_ref, alog_ref,
                dexp_ref, nw_ref, e_ref, shift_ref, y_ref, tail_ref, st_ref, *, chunks_per_step):
    q = SSD_CHUNK
    hg = SSD_HEADS_PER_GROUP
    p = SSD_HEAD_DIM
    gw = SSD_GROUP_W
    n = SSD_D_STATE

    @pl.when(pl.program_id(2) == 0)
    def _():
        tail_ref[...] = jnp.zeros_like(tail_ref)
        st_ref[...] = jnp.zeros_like(st_ref)

    row = lax.broadcasted_iota(jnp.int32, (q, q), 0)
    col = lax.broadcasted_iota(jnp.int32, (q, q), 1)
    causal = col <= row
    tri_b = causal.astype(BF16)
    even_lanes = col < p
    nt = (((1,), (1,)), ((), ()))

    cw = cw_ref[...]
    cbias = cb_ref[...]
    dtb = dtb_ref[...]
    a_neg = -jnp.exp(alog_ref[...])
    dexp = dexp_ref[...]
    nw = nw_ref[...]
    shift = shift_ref[...]

    st = st_ref[...]
    prev_raw = tail_ref[...]
    for cc in range(chunks_per_step):
        rows = pl.ds(cc * q, q)
        raw = jnp.concatenate([x_ref[rows, :], b_ref[rows, :], c_ref[rows, :]], axis=1)
        sh = jnp.dot(shift, jnp.concatenate([prev_raw, raw], axis=0),
                     preferred_element_type=F32)
        acc = cbias + cw[3:4, :] * raw.astype(F32)
        for s in range(1, SSD_CONV_W):
            acc = acc + cw[3 - s:4 - s, :] * sh[(s - 1) * q:s * q, :]
        prev_raw = raw
        xbc = _silu(acc)
        xs = xbc[:, 0:gw]
        bm_b = xbc[:, gw:gw + n].astype(BF16)
        cm_b = xbc[:, gw + n:].astype(BF16)

        dt_t = _softplus(dt_ref[:, cc * q:(cc + 1) * q] + dtb)
        a_cs_t = sum(lax.dot_general(part, tri_b, nt, preferred_element_type=F32)
                     for part in _split3(dt_t * a_neg))
        a_last = a_cs_t[:, q - 1:q]
        stack = jnp.concatenate(
            [a_cs_t, dt_t, jnp.exp(a_cs_t), jnp.exp(a_last - a_cs_t),
             jnp.broadcast_to(jnp.exp(a_last), (hg, q)),
             jnp.zeros((q - (_EXP_SECTIONS + _EXP_SKIP) * hg, q), F32)], axis=0)
        cs = stack.T
        hi = cs.astype(BF16)
        lo = (cs - hi.astype(F32)).astype(BF16)
        hilo = jnp.concatenate([hi, lo], axis=1)
        ex = jnp.dot(hilo, e_ref[:, 0:3 * gw], preferred_element_type=F32)
        dt_exp = ex[:, 0:gw]
        ea_exp = ex[:, gw:2 * gw]
        w_exp = ex[:, 2 * gw:3 * gw]
        el_exp = jnp.dot(hilo[0:hg, :], e_ref[:, 3 * gw:4 * gw],
                         preferred_element_type=F32)[0:1, :]

        xdt = xs * dt_exp
        xdt_b = xdt.astype(BF16)
        xw_b = (xdt * w_exp).astype(BF16)
        cb = lax.dot_general(cm_b, bm_b, nt, preferred_element_type=F32)
        cbm = jnp.where(causal, cb, 0.0)
        y_off = jnp.dot(cm_b, st.astype(BF16), preferred_element_type=F32) * ea_exp
        st = st * el_exp + lax.dot_general(
            bm_b, xw_b, (((0,), (0,)), ((), ())), preferred_element_type=F32)

        ys = []
        for hp in range(hg // 2):
            l_mats = []
            for hh in (2 * hp, 2 * hp + 1):
                seg = cs[:, hh:hh + 1] - a_cs_t[hh:hh + 1, :]
                l_mats.append((cbm * jnp.exp(jnp.minimum(seg, 0.0))).astype(BF16))
            x_pair = xdt_b[:, 2 * hp * p:(2 * hp + 2) * p]
            zero = jnp.zeros_like(x_pair)
            rhs = jnp.concatenate([jnp.where(even_lanes, x_pair, zero),
                                   jnp.where(even_lanes, zero, x_pair)], axis=0)
            ys.append(jnp.dot(jnp.concatenate(l_mats, axis=1), rhs, preferred_element_type=F32))

        y = jnp.concatenate(ys, axis=1) + y_off + dexp * xs
        y = y * _silu(z_ref[rows, :].astype(F32))
        y = y * lax.rsqrt(jnp.mean(y * y, axis=-1, keepdims=True) + RMS_EPS) * nw
        y_ref[rows, :] = y.astype(y_ref.dtype)

    st_ref[...] = st
    tail_ref[...] = prev_raw


def _ssd(proj, small_t, cw_g, cb_g, dtb_g, alog_g, dexp_g, nw_g, *, batch, seq, chunks_per_step=4):
    rows = SSD_CHUNK * chunks_per_step
    steps = seq // rows
    gw = SSD_GROUP_W
    n = SSD_D_STATE
    hg = SSD_HEADS_PER_GROUP
    e_mat = _ssd_expand_matrix()
    shift = _ssd_shift_matrix()

    def row_blk(b, c):
        return b * steps + c

    return pl.pallas_call(
        functools.partial(_ssd_kernel, chunks_per_step=chunks_per_step),
        grid=(batch, SSD_N_GROUPS, steps),
        in_specs=[
            pl.BlockSpec((rows, gw), lambda b, g, c: (row_blk(b, c), OFF_X // gw + g)),
            pl.BlockSpec((rows, n), lambda b, g, c: (row_blk(b, c), OFF_B // n + g)),
            pl.BlockSpec((rows, n), lambda b, g, c: (row_blk(b, c), OFF_C // n + g)),
            pl.BlockSpec((rows, gw), lambda b, g, c: (row_blk(b, c), OFF_Z // gw + g)),
            pl.BlockSpec((hg, rows), lambda b, g, c: (g, row_blk(b, c))),
            pl.BlockSpec((None, SSD_CONV_W, SSD_GROUP_CONV_W), lambda b, g, c: (g, 0, 0)),
            pl.BlockSpec((None, 1, SSD_GROUP_CONV_W), lambda b, g, c: (g, 0, 0)),
            pl.BlockSpec((None, hg, LANE), lambda b, g, c: (g, 0, 0)),
            pl.BlockSpec((None, hg, LANE), lambda b, g, c: (g, 0, 0)),
            pl.BlockSpec((None, 1, gw), lambda b, g, c: (g, 0, 0)),
            pl.BlockSpec((None, 1, gw), lambda b, g, c: (g, 0, 0)),
            pl.BlockSpec(e_mat.shape, lambda b, g, c: (0, 0)),
            pl.BlockSpec(shift.shape, lambda b, g, c: (0, 0)),
        ],
        out_specs=pl.BlockSpec((rows, gw), lambda b, g, c: (row_blk(b, c), g)),
        out_shape=jax.ShapeDtypeStruct((batch * seq, SSD_D_INNER), BF16),
        scratch_shapes=[pltpu.VMEM((SSD_CHUNK, SSD_GROUP_CONV_W), BF16),
                        pltpu.VMEM((SSD_D_STATE, gw), F32)],
        compiler_params=_cparams(3),
    )(proj, proj, proj, proj, small_t, cw_g, cb_g, dtb_g, alog_g, dexp_g, nw_g, e_mat, shift)


def _gla_cumsum_matrix(rows):
    r = lax.broadcasted_iota(jnp.int32, (rows, rows), 0)
    c = lax.broadcasted_iota(jnp.int32, (rows, rows), 1)
    return jnp.logical_and(r // GLA_CHUNK == c // GLA_CHUNK, c <= r).astype(F32)


def _gla_kernel(q_ref, k_ref, v_ref, g_ref, sm_ref, gkw_ref, gkb_ref, nw_ref, tri_ref, trib_ref,
                o_ref, st_ref, *, chunks_per_step):
    c = GLA_CHUNK
    scale = GLA_HEAD_K ** -0.5
    nt = (((1,), (1,)), ((), ()))

    @pl.when(pl.program_id(2) == 0)
    def _():
        st_ref[...] = jnp.zeros_like(st_ref)

    tri = tri_ref[...]
    tri_b = trib_ref[...]
    s_hi, s_lo, _ = _split3(sm_ref[...])
    w_hi, w_lo, _ = _split3(gkw_ref[...])
    pre = (jnp.dot(s_hi, w_hi, preferred_element_type=F32)
           + jnp.dot(s_hi, w_lo, preferred_element_type=F32)
           + jnp.dot(s_lo, w_hi, preferred_element_type=F32)) + gkb_ref[...]
    gk = -_softplus(-pre) * (1.0 / GLA_GATE_NORMALIZER)
    gcs = sum(jnp.dot(tri_b, part, preferred_element_type=F32)
              for part in _split3(gk))
    qf = q_ref[...].astype(F32)
    kf = k_ref[...].astype(F32)
    vb = v_ref[...]
    qg = (qf * scale * jnp.exp(gcs)).astype(BF16)
    kg = (kf * jnp.exp(-gcs)).astype(BF16)
    att = lax.dot_general(qg, kg, nt, preferred_element_type=F32) * tri
    o_intra = jnp.dot(att.astype(BF16), vb, preferred_element_type=F32)

    st = st_ref[...]
    outs = []
    for cc in range(chunks_per_step):
        rs = slice(cc * c, (cc + 1) * c)
        g_last = gcs[(cc + 1) * c - 1:(cc + 1) * c, :]
        kd = (kf[rs] * jnp.exp(g_last - gcs[rs])).astype(BF16)
        outs.append(o_intra[rs] + lax.dot_general(qg[rs], st.astype(BF16), nt,
                                                  preferred_element_type=F32))
        st = st * jnp.exp(g_last) + lax.dot_general(
            vb[rs], kd, (((0,), (0,)), ((), ())), preferred_element_type=F32)
    st_ref[...] = st

    o = jnp.concatenate(outs, axis=0)
    o = o * lax.rsqrt(jnp.mean(o * o, axis=-1, keepdims=True) + RMS_EPS) * nw_ref[...]
    o_ref[...] = (o * _silu(g_ref[...].astype(F32))).astype(o_ref.dtype)


def _gla(proj, small, gkw_pad, gkb, nw, *, batch, seq, chunks_per_step=4):
    rows = GLA_CHUNK * chunks_per_step
    steps = seq // rows
    hk, hv = GLA_HEAD_K, GLA_HEAD_V
    tri = _gla_cumsum_matrix(rows)

    def row_blk(b, c):
        return b * steps + c

    return pl.pallas_call(
        functools.partial(_gla_kernel, chunks_per_step=chunks_per_step),
        grid=(batch, GLA_N_HEADS, steps),
        in_specs=[
            pl.BlockSpec((rows, hk), lambda b, h, c: (row_blk(b, c), OFF_Q // hk + h)),
            pl.BlockSpec((rows, hk), lambda b, h, c: (row_blk(b, c), OFF_K // hk + h)),
            pl.BlockSpec((rows, hv), lambda b, h, c: (row_blk(b, c), OFF_V // hv + h)),
            pl.BlockSpec((rows, hv), lambda b, h, c: (row_blk(b, c), OFF_G // hv + h)),
            pl.BlockSpec((rows, LANE), lambda b, h, c: (row_blk(b, c), 0)),
            pl.BlockSpec((LANE, hk), lambda b, h, c: (0, h)),
            pl.BlockSpec((1, hk), lambda b, h, c: (0, h)),
            pl.BlockSpec((1, hv), lambda b, h, c: (0, 0)),
            pl.BlockSpec((rows, rows), lambda b, h, c: (0, 0)),
            pl.BlockSpec((rows, rows), lambda b, h, c: (0, 0)),
        ],
        out_specs=pl.BlockSpec((rows, hv), lambda b, h, c: (row_blk(b, c), h)),
        out_shape=jax.ShapeDtypeStruct((batch * seq, GLA_D_VALUE), BF16),
        scratch_shapes=[pltpu.VMEM((hv, hk), F32)],
        compiler_params=_cparams(3),
    )(proj, proj, proj, proj, small, gkw_pad, gkb, nw, tri, tri.astype(BF16))


def _split_in_proj(w_in):
    c0 = SSD_D_INNER + SSD_D_INNER + 2 * SSD_N_GROUPS * SSD_D_STATE
    c1 = c0 + SSD_N_HEADS
    c2 = c1 + 2 * GLA_D_KEY + 2 * GLA_D_VALUE
    c3 = c2 + GLA_GATE_RANK
    w_main = jnp.concatenate([w_in[:, :c0], w_in[:, c1:c2], w_in[:, c3:]], axis=1)
    pad = jnp.zeros((w_in.shape[0], LANE - SSD_N_HEADS - GLA_GATE_RANK), w_in.dtype)
    w_small = jnp.concatenate([w_in[:, c0:c1], w_in[:, c2:c3], pad], axis=1)
    return w_main.astype(BF16), w_small.astype(BF16)


def _head_rows(v):
    vg = v.reshape(SSD_N_GROUPS, SSD_HEADS_PER_GROUP, 1)
    return jnp.broadcast_to(vg, (SSD_N_GROUPS, SSD_HEADS_PER_GROUP, LANE))


def _layer(xf, xb, batch, seq, w_in, conv_w, conv_b, dt_bias, a_log, d_skip, ssd_norm_w,
           gk_w, gk_b, gla_norm_w, w_ssd, w_gla, gate_bias, w_out, ln1_g, ln1_b,
           w_up, w_down, ln2_g, ln2_b):
    w_main, w_small = _split_in_proj(w_in)
    proj = _matmul(xb, w_main, tm=1024, tn=1024, out_dtype=BF16)
    small, small_t = _small_proj(xb, w_small)

    gw, n = SSD_GROUP_W, SSD_D_STATE
    ng = SSD_N_GROUPS

    def conv_cols(v):
        lead = v.shape[:-1]
        xs = v[..., :SSD_D_INNER].reshape(lead + (ng, gw))
        bs = v[..., SSD_D_INNER:SSD_D_INNER + ng * n].reshape(lead + (ng, n))
        cs = v[..., SSD_D_INNER + ng * n:].reshape(lead + (ng, n))
        return jnp.moveaxis(jnp.concatenate([xs, bs, cs], axis=-1), -2, 0)

    cw_g = conv_cols(conv_w)
    cb_g = conv_cols(conv_b[None, :])
    dtb_g = _head_rows(dt_bias)
    alog_g = _head_rows(a_log)
    dexp_g = jnp.repeat(d_skip, SSD_HEAD_DIM).reshape(ng, 1, gw)
    nw_g = ssd_norm_w.reshape(ng, 1, gw)
    y = _ssd(proj, small_t, cw_g, cb_g, dtb_g, alog_g, dexp_g, nw_g, batch=batch, seq=seq)

    gkw_pad = jnp.zeros((LANE, GLA_D_KEY), F32).at[SMALL_GK:SMALL_GK + GLA_GATE_RANK].set(gk_w)
    o = _gla(proj, small, gkw_pad, gk_b[None, :], gla_norm_w[None, :], batch=batch, seq=seq)

    merged = _merge(y, o, proj, w_ssd.astype(BF16), w_gla.astype(BF16), gate_bias)
    hf, hb = _proj_ln(merged, w_out.astype(BF16), xf, ln1_g[None, :], ln1_b[None, :])
    u = _matmul(hb, w_up.astype(BF16), tm=1024, tn=1024, out_dtype=BF16, act="relu2")
    return _down_ln(u, w_down.astype(BF16), hf, ln2_g[None, :], ln2_b[None, :])


def kernel(x, w_in, ssd_conv_w, ssd_conv_b, ssd_dt_bias, ssd_A_log, ssd_D, ssd_norm_w, gla_gk_w, gla_gk_b, gla_norm_w, w_ssd_branch, w_gla_branch, gate_bias, w_out, ln1_g, ln1_b, w_up, w_down, ln2_g, ln2_b):
    batch, seq, d = x.shape
    xf = x.reshape(batch * seq, d)
    xb = xf.astype(BF16)
    for l in range(w_in.shape[0]):
        xf, xb = _layer(xf, xb, batch, seq, w_in[l], ssd_conv_w[l], ssd_conv_b[l], ssd_dt_bias[l],
                        ssd_A_log[l], ssd_D[l], ssd_norm_w[l], gla_gk_w[l], gla_gk_b[l],
                        gla_norm_w[l], w_ssd_branch[l], w_gla_branch[l], gate_bias[l], w_out[l],
                        ln1_g[l], ln1_b[l], w_up[l], w_down[l], ln2_g[l], ln2_b[l])
    return xf.reshape(batch, seq, d)
```

```python
import functools

import jax
import jax.numpy as jnp
from jax import lax
from jax.experimental import pallas as pl
from jax.experimental.pallas import tpu as pltpu

F32 = jnp.float32
BF16 = jnp.bfloat16
HIGHEST = lax.Precision.HIGHEST

D_MODEL = 2048
DEPTH = 4
ALPHA = (2.0 * DEPTH) ** 0.25
LN_EPS = 1e-5
RMS_EPS = 1e-5

SSD_D_INNER = 2 * D_MODEL
SSD_HEAD_DIM = 64
SSD_N_HEADS = SSD_D_INNER // SSD_HEAD_DIM
SSD_N_GROUPS = 8
SSD_HEADS_PER_GROUP = SSD_N_HEADS // SSD_N_GROUPS
SSD_D_STATE = 128
SSD_CONV_W = 4
SSD_CHUNK = 128
SSD_GROUP_W = SSD_HEADS_PER_GROUP * SSD_HEAD_DIM
SSD_GROUP_CONV_W = SSD_GROUP_W + 2 * SSD_D_STATE

GLA_N_HEADS = 4
GLA_D_KEY = D_MODEL // 2
GLA_D_VALUE = D_MODEL
GLA_HEAD_K = GLA_D_KEY // GLA_N_HEADS
GLA_HEAD_V = GLA_D_VALUE // GLA_N_HEADS
GLA_GATE_RANK = 16
GLA_GATE_NORMALIZER = 16.0
GLA_CHUNK = 64

D_FF = 4 * D_MODEL

OFF_Z = 0
OFF_X = OFF_Z + SSD_D_INNER
OFF_B = OFF_X + SSD_D_INNER
OFF_C = OFF_B + SSD_N_GROUPS * SSD_D_STATE
OFF_Q = OFF_C + SSD_N_GROUPS * SSD_D_STATE
OFF_K = OFF_Q + GLA_D_KEY
OFF_V = OFF_K + GLA_D_KEY
OFF_G = OFF_V + GLA_D_VALUE
OFF_GS = OFF_G + GLA_D_VALUE
OFF_GG = OFF_GS + D_MODEL
N_MAIN = OFF_GG + D_MODEL
LANE = 128
SMALL_DT = 0
SMALL_GK = SSD_N_HEADS

VMEM_LIMIT = 56 * 1024 * 1024


def _cparams(n_axes):
    return pltpu.CompilerParams(dimension_semantics=("arbitrary",) * n_axes,
                                vmem_limit_bytes=VMEM_LIMIT)


def _sigmoid(x):
    return 0.5 + 0.5 * jnp.tanh(0.5 * x)


def _silu(x):
    h = 0.5 * x
    return h + h * jnp.tanh(h)


def _softplus(x):
    return jnp.maximum(x, 0.0) + jnp.log(1.0 + jnp.exp(-jnp.abs(x)))


def _layer_norm(v, g, b):
    mu = jnp.mean(v, axis=-1, keepdims=True)
    vc = v - mu
    var = jnp.mean(vc * vc, axis=-1, keepdims=True)
    return vc * lax.rsqrt(var + LN_EPS) * g + b


CAST_ROWS = 256


def _cast_weight(dst_ref, src_ref):
    def body(r, carry):
        rows = pl.ds(pl.multiple_of(r * CAST_ROWS, CAST_ROWS), CAST_ROWS)
        dst_ref[rows, :] = src_ref[rows, :].astype(BF16)
        return carry
    lax.fori_loop(0, src_ref.shape[0] // CAST_ROWS, body, 0)


def _mm_kernel(a_ref, w_ref, o_ref, wt_ref, *, act):
    @pl.when(pl.program_id(1) == 0)
    def _():
        _cast_weight(wt_ref, w_ref)

    acc = jnp.dot(a_ref[...], wt_ref[...], preferred_element_type=F32)
    if act == "relu2":
        r = jnp.maximum(acc, 0.0)
        acc = r * r
    o_ref[...] = acc.astype(o_ref.dtype)


def _matmul(a, w, layer, *, tm, tn, out_dtype, act=None):
    m, k = a.shape
    n = w.shape[2]
    return pl.pallas_call(
        functools.partial(_mm_kernel, act=act),
        grid=(n // tn, m // tm),
        in_specs=[pl.BlockSpec((tm, k), lambda j, i: (i, 0)),
                  pl.BlockSpec((None, k, tn), lambda j, i: (layer, 0, j))],
        out_specs=pl.BlockSpec((tm, tn), lambda j, i: (i, j)),
        out_shape=jax.ShapeDtypeStruct((m, n), out_dtype),
        scratch_shapes=[pltpu.VMEM((k, tn), BF16)],
        compiler_params=_cparams(2),
    )(a, w)


IN_SHIFT_QKVG = SSD_N_HEADS
IN_SHIFT_GATES = SSD_N_HEADS + GLA_GATE_RANK


def _in_proj_kernel(a_ref, w_ref, o_ref, wt_ref):
    @pl.when(pl.program_id(1) == 0)
    def _():
        _cast_weight(wt_ref, w_ref.at[0])

    o_ref[...] = lax.dot_general(a_ref[...], wt_ref[...], (((1,), (1,)), ((), ())),
                                 preferred_element_type=F32).astype(o_ref.dtype)


def _in_proj(a, w_in_t, layer, *, tm=1024, tn=1024):
    m, k = a.shape
    assert OFF_Q % tn == 0 and OFF_GS % tn == 0 and N_MAIN % tn == 0

    def w_rows(j, i):
        past_dt = (j >= OFF_Q // tn).astype(jnp.int32)
        past_gk = (j >= OFF_GS // tn).astype(jnp.int32)
        shift = past_dt * IN_SHIFT_QKVG + past_gk * (IN_SHIFT_GATES - IN_SHIFT_QKVG)
        return layer, pl.multiple_of(j * tn + shift, 16), 0

    return pl.pallas_call(
        _in_proj_kernel,
        grid=(N_MAIN // tn, m // tm),
        in_specs=[pl.BlockSpec((tm, k), lambda j, i: (i, 0)),
                  pl.BlockSpec((pl.Element(1), pl.Element(tn), pl.Element(k)), w_rows)],
        out_specs=pl.BlockSpec((tm, tn), lambda j, i: (i, j)),
        out_shape=jax.ShapeDtypeStruct((m, N_MAIN), BF16),
        scratch_shapes=[pltpu.VMEM((tn, k), BF16)],
        compiler_params=_cparams(2),
    )(a, w_in_t)


def _small_proj_kernel(a_ref, b_ref, o_ref, ot_ref):
    acc = lax.dot_general(a_ref[...], b_ref[...].astype(BF16), (((1,), (1,)), ((), ())),
                          preferred_element_type=F32)
    o_ref[...] = acc
    ot_ref[...] = acc.T


def _small_proj(a, b_t, layer, *, tm=1024):
    m, k = a.shape
    return pl.pallas_call(
        _small_proj_kernel,
        grid=(m // tm,),
        in_specs=[pl.BlockSpec((tm, k), lambda i: (i, 0)),
                  pl.BlockSpec((None, LANE, k), lambda i: (layer, 0, 0))],
        out_specs=[pl.BlockSpec((tm, LANE), lambda i: (i, 0)),
                   pl.BlockSpec((LANE, tm), lambda i: (0, i))],
        out_shape=[jax.ShapeDtypeStruct((m, LANE), F32),
                   jax.ShapeDtypeStruct((LANE, m), F32)],
        compiler_params=_cparams(1),
    )(a, b_t)


def _merge_kernel(y_ref, o_ref, ws_ref, wg_ref, gs_ref, gg_ref, gb_ref, out_ref, wst_ref, wgt_ref):
    @pl.when(pl.program_id(1) == 0)
    def _():
        _cast_weight(wst_ref, ws_ref)
        _cast_weight(wgt_ref, wg_ref)

    bs = jnp.dot(y_ref[...], wst_ref[...], preferred_element_type=F32)
    bg = jnp.dot(o_ref[...], wgt_ref[...], preferred_element_type=F32)
    gb = gb_ref[...]
    s_gate = _sigmoid(gs_ref[...].astype(F32) + gb[0:1, :])
    g_gate = _sigmoid(gg_ref[...].astype(F32) + gb[1:2, :])
    out_ref[...] = (s_gate * bs + g_gate * bg).astype(out_ref.dtype)


def _merge(y, o, proj, w_ssd, w_gla, gate_bias, layer, *, tm=512, tn=512):
    t = y.shape[0]
    return pl.pallas_call(
        _merge_kernel,
        grid=(D_MODEL // tn, t // tm),
        in_specs=[pl.BlockSpec((tm, SSD_D_INNER), lambda j, i: (i, 0)),
                  pl.BlockSpec((tm, GLA_D_VALUE), lambda j, i: (i, 0)),
                  pl.BlockSpec((None, SSD_D_INNER, tn), lambda j, i: (layer, 0, j)),
                  pl.BlockSpec((None, GLA_D_VALUE, tn), lambda j, i: (layer, 0, j)),
                  pl.BlockSpec((tm, tn), lambda j, i: (i, OFF_GS // tn + j)),
                  pl.BlockSpec((tm, tn), lambda j, i: (i, OFF_GG // tn + j)),
                  pl.BlockSpec((None, 2, tn), lambda j, i: (layer, 0, j))],
        out_specs=pl.BlockSpec((tm, tn), lambda j, i: (i, j)),
        out_shape=jax.ShapeDtypeStruct((t, D_MODEL), BF16),
        scratch_shapes=[pltpu.VMEM((SSD_D_INNER, tn), BF16), pltpu.VMEM((GLA_D_VALUE, tn), BF16)],
        compiler_params=_cparams(2),
    )(y, o, w_ssd, w_gla, proj, proj, gate_bias)


def _proj_ln_kernel(a_ref, w_ref, res_ref, g_ref, b_ref, of_ref, ob_ref):
    acc = jnp.dot(a_ref[...], w_ref[...], preferred_element_type=F32)
    out = _layer_norm(ALPHA * res_ref[...] + acc, g_ref[...], b_ref[...])
    of_ref[...] = out
    ob_ref[...] = out.astype(BF16)


def _proj_ln(a, w, res, g, b, layer, *, tm=512):
    t, k = a.shape
    return pl.pallas_call(
        _proj_ln_kernel,
        grid=(t // tm,),
        in_specs=[pl.BlockSpec((tm, k), lambda i: (i, 0)),
                  pl.BlockSpec((None, k, D_MODEL), lambda i: (layer, 0, 0)),
                  pl.BlockSpec((tm, D_MODEL), lambda i: (i, 0)),
                  pl.BlockSpec((None, 1, D_MODEL), lambda i: (layer, 0, 0)),
                  pl.BlockSpec((None, 1, D_MODEL), lambda i: (layer, 0, 0))],
        out_specs=[pl.BlockSpec((tm, D_MODEL), lambda i: (i, 0)),
                   pl.BlockSpec((tm, D_MODEL), lambda i: (i, 0))],
        out_shape=[jax.ShapeDtypeStruct((t, D_MODEL), F32),
                   jax.ShapeDtypeStruct((t, D_MODEL), BF16)],
        compiler_params=_cparams(1),
    )(a, w, res, g, b)


def _down_ln_kernel(u_ref, w_ref, res_ref, g_ref, b_ref, of_ref, ob_ref, acc_ref):
    kk = pl.program_id(1)

    @pl.when(kk == 0)
    def _():
        acc_ref[...] = jnp.zeros_like(acc_ref)

    acc_ref[...] += jnp.dot(u_ref[...], w_ref[...], preferred_element_type=F32)

    @pl.when(kk == pl.num_programs(1) - 1)
    def _():
        out = _layer_norm(ALPHA * res_ref[...] + acc_ref[...], g_ref[...], b_ref[...])
        of_ref[...] = out
        ob_ref[...] = out.astype(BF16)


def _down_ln(u, w, res, g, b, layer, *, tm=512, tk=2048):
    t, k = u.shape
    return pl.pallas_call(
        _down_ln_kernel,
        grid=(t // tm, k // tk),
        in_specs=[pl.BlockSpec((tm, tk), lambda i, kk: (i, kk)),
                  pl.BlockSpec((None, tk, D_MODEL), lambda i, kk: (layer, kk, 0)),
                  pl.BlockSpec((tm, D_MODEL), lambda i, kk: (i, 0)),
                  pl.BlockSpec((None, 1, D_MODEL), lambda i, kk: (layer, 0, 0)),
                  pl.BlockSpec((None, 1, D_MODEL), lambda i, kk: (layer, 0, 0))],
        out_specs=[pl.BlockSpec((tm, D_MODEL), lambda i, kk: (i, 0)),
                   pl.BlockSpec((tm, D_MODEL), lambda i, kk: (i, 0))],
        out_shape=[jax.ShapeDtypeStruct((t, D_MODEL), F32),
                   jax.ShapeDtypeStruct((t, D_MODEL), BF16)],
        scratch_shapes=[pltpu.VMEM((tm, D_MODEL), F32)],
        compiler_params=_cparams(2),
    )(u, w, res, g, b)


_EXP_SKIP = 1
_EXP_SECTIONS = 4


def _ssd_expand_matrix():
    r = lax.broadcasted_iota(jnp.int32, (LANE, _EXP_SECTIONS * SSD_GROUP_W), 0)
    c = lax.broadcasted_iota(jnp.int32, (LANE, _EXP_SECTIONS * SSD_GROUP_W), 1)
    src = (c // SSD_GROUP_W + _EXP_SKIP) * SSD_HEADS_PER_GROUP + (c % SSD_GROUP_W) // SSD_HEAD_DIM
    e = (r == src).astype(BF16)
    return jnp.concatenate([e, e], axis=0)


def _ssd_shift_matrix():
    q = SSD_CHUNK
    r = lax.broadcasted_iota(jnp.int32, (3 * q, 2 * q), 0)
    c = lax.broadcasted_iota(jnp.int32, (3 * q, 2 * q), 1)
    return (c == q + r % q - (r // q + 1)).astype(BF16)


def _split3(v):
    hi = v.astype(BF16)
    r1 = v - hi.astype(F32)
    mid = r1.astype(BF16)
    lo = (r1 - mid.astype(F32)).astype(BF16)
    return hi, mid, lo


def _ssd_kernel(x_ref, b_ref, c_ref, z_ref, dt_ref, cw_ref, cb_ref, dtb_ref, alog_ref,
                dexp_ref, nw_ref, e_ref, shift_ref, y_ref, tail_ref, st_ref, *, chunks_per_step):
    q = SSD_CHUNK
    hg = SSD_HEADS_PER_GROUP
    p = SSD_HEAD_DIM
    gw = SSD_GROUP_W
    n = SSD_D_STATE

    @pl.when(pl.program_id(2) == 0)
    def _():
        tail_ref[...] = jnp.zeros_like(tail_ref)
        st_ref[...] = jnp.zeros_like(st_ref)

    row = lax.broadcasted_iota(jnp.int32, (q, q), 0)
    col = lax.broadcasted_iota(jnp.int32, (q, q), 1)
    causal = col <= row
    tri_b = causal.astype(BF16)
    even_lanes = col < p
    nt = (((1,), (1,)), ((), ()))

    cw = cw_ref[...]
    cbias = cb_ref[...]
    dtb = dtb_ref[...]
    a_neg = -jnp.exp(alog_ref[...])
    dexp = dexp_ref[...]
    nw = nw_ref[...]
    shift = shift_ref[...]

    def dot(a, b, dims=None):
        if dims is None:
            return jnp.dot(a, b, preferred_element_type=F32)
        return lax.dot_general(a, b, dims, preferred_element_type=F32)

    cs_ = range(chunks_per_step)
    rows = [pl.ds(cc * q, q) for cc in cs_]

    raw = [jnp.concatenate([x_ref[r, :], b_ref[r, :], c_ref[r, :]], axis=1) for r in rows]
    prev = [tail_ref[...]] + raw[:-1]
    sh = [dot(shift, jnp.concatenate([prev[cc], raw[cc]], axis=0)) for cc in cs_]
    xbc = []
    for cc in cs_:
        acc = cbias + cw[3:4, :] * raw[cc].astype(F32)
        for s in range(1, SSD_CONV_W):
            acc = acc + cw[3 - s:4 - s, :] * sh[cc][(s - 1) * q:s * q, :]
        xbc.append(_silu(acc))
    xs = [v[:, 0:gw] for v in xbc]
    bm_b = [v[:, gw:gw + n].astype(BF16) for v in xbc]
    cm_b = [v[:, gw + n:].astype(BF16) for v in xbc]

    dt_t = [_softplus(dt_ref[:, cc * q:(cc + 1) * q] + dtb) for cc in cs_]
    a_cs_t = [sum(dot(part, tri_b, nt) for part in _split3(v * a_neg))
              for v in dt_t]
    cs = []
    for cc in cs_:
        a_last = a_cs_t[cc][:, q - 1:q]
        stack = jnp.concatenate(
            [a_cs_t[cc], dt_t[cc], jnp.exp(a_cs_t[cc]), jnp.exp(a_last - a_cs_t[cc]),
             jnp.broadcast_to(jnp.exp(a_last), (hg, q)),
             jnp.zeros((q - (_EXP_SECTIONS + _EXP_SKIP) * hg, q), F32)], axis=0)
        cs.append(stack.T)
    hilo = []
    for v in cs:
        hi = v.astype(BF16)
        hilo.append(jnp.concatenate([hi, (v - hi.astype(F32)).astype(BF16)], axis=1))
    ex = [dot(v, e_ref[:, 0:3 * gw]) for v in hilo]
    el_exp = [dot(v[0:hg, :], e_ref[:, 3 * gw:4 * gw])[0:1, :] for v in hilo]

    xdt = [xs[cc] * ex[cc][:, 0:gw] for cc in cs_]
    xdt_b = [v.astype(BF16) for v in xdt]
    xw_b = [(xdt[cc] * ex[cc][:, 2 * gw:3 * gw]).astype(BF16) for cc in cs_]
    cbm = [jnp.where(causal, dot(cm_b[cc], bm_b[cc], nt), 0.0) for cc in cs_]
    states = [dot(bm_b[cc], xw_b[cc], (((0,), (0,)), ((), ()))) for cc in cs_]

    y_diag = []
    for cc in cs_:
        ys = []
        for hp in range(hg // 2):
            l_mats = []
            for hh in (2 * hp, 2 * hp + 1):
                seg = cs[cc][:, hh:hh + 1] - a_cs_t[cc][hh:hh + 1, :]
                l_mats.append((cbm[cc] * jnp.exp(jnp.minimum(seg, 0.0))).astype(BF16))
            x_pair = xdt_b[cc][:, 2 * hp * p:(2 * hp + 2) * p]
            zero = jnp.zeros_like(x_pair)
            rhs = jnp.concatenate([jnp.where(even_lanes, x_pair, zero),
                                   jnp.where(even_lanes, zero, x_pair)], axis=0)
            ys.append(dot(jnp.concatenate(l_mats, axis=1), rhs))
        y_diag.append(jnp.concatenate(ys, axis=1))

    st = st_ref[...]
    y_off = []
    for cc in cs_:
        y_off.append(dot(cm_b[cc], st.astype(BF16)) * ex[cc][:, gw:2 * gw])
        st = st * el_exp[cc] + states[cc]
    st_ref[...] = st
    tail_ref[...] = raw[-1]

    for cc in cs_:
        y = y_diag[cc] + y_off[cc] + dexp * xs[cc]
        y = y * _silu(z_ref[rows[cc], :].astype(F32))
        y = y * lax.rsqrt(jnp.mean(y * y, axis=-1, keepdims=True) + RMS_EPS) * nw
        y_ref[rows[cc], :] = y.astype(y_ref.dtype)


def _ssd(proj, small_t, cw_g, cb_g, dtb_g, alog_g, dexp_g, nw_g, *, batch, seq, chunks_per_step=4):
    rows = SSD_CHUNK * chunks_per_step
    steps = seq // rows
    gw = SSD_GROUP_W
    n = SSD_D_STATE
    hg = SSD_HEADS_PER_GROUP
    e_mat = _ssd_expand_matrix()
    shift = _ssd_shift_matrix()

    def row_blk(b, c):
        return b * steps + c

    return pl.pallas_call(
        functools.partial(_ssd_kernel, chunks_per_step=chunks_per_step),
        grid=(batch, SSD_N_GROUPS, steps),
        in_specs=[
            pl.BlockSpec((rows, gw), lambda b, g, c: (row_blk(b, c), OFF_X // gw + g)),
            pl.BlockSpec((rows, n), lambda b, g, c: (row_blk(b, c), OFF_B // n + g)),
            pl.BlockSpec((rows, n), lambda b, g, c: (row_blk(b, c), OFF_C // n + g)),
            pl.BlockSpec((rows, gw), lambda b, g, c: (row_blk(b, c), OFF_Z // gw + g)),
            pl.BlockSpec((hg, rows), lambda b, g, c: (g, row_blk(b, c))),
            pl.BlockSpec((None, SSD_CONV_W, SSD_GROUP_CONV_W), lambda b, g, c: (g, 0, 0)),
            pl.BlockSpec((None, 1, SSD_GROUP_CONV_W), lambda b, g, c: (g, 0, 0)),
            pl.BlockSpec((None, hg, LANE), lambda b, g, c: (g, 0, 0)),
            pl.BlockSpec((None, hg, LANE), lambda b, g, c: (g, 0, 0)),
            pl.BlockSpec((None, 1, gw), lambda b, g, c: (g, 0, 0)),
            pl.BlockSpec((None, 1, gw), lambda b, g, c: (g, 0, 0)),
            pl.BlockSpec(e_mat.shape, lambda b, g, c: (0, 0)),
            pl.BlockSpec(shift.shape, lambda b, g, c: (0, 0)),
        ],
        out_specs=pl.BlockSpec((rows, gw), lambda b, g, c: (row_blk(b, c), g)),
        out_shape=jax.ShapeDtypeStruct((batch * seq, SSD_D_INNER), BF16),
        scratch_shapes=[pltpu.VMEM((SSD_CHUNK, SSD_GROUP_CONV_W), BF16),
                        pltpu.VMEM((SSD_D_STATE, gw), F32)],
        compiler_params=_cparams(3),
    )(proj, proj, proj, proj, small_t, cw_g, cb_g, dtb_g, alog_g, dexp_g, nw_g, e_mat, shift)


def _gla_cumsum_matrix(rows):
    r = lax.broadcasted_iota(jnp.int32, (rows, rows), 0)
    c = lax.broadcasted_iota(jnp.int32, (rows, rows), 1)
    return jnp.logical_and(r // GLA_CHUNK == c // GLA_CHUNK, c <= r).astype(F32)


def _gla_kernel(q_ref, k_ref, v_ref, g_ref, sm_ref, gkw_ref, gkb_ref, nw_ref, tri_ref, trib_ref,
                o_ref, st_ref, *, chunks_per_step):
    @pl.when(pl.program_id(1) == 0)
    def _():
        st_ref[...] = jnp.zeros_like(st_ref)

    c = GLA_CHUNK
    scale = GLA_HEAD_K ** -0.5
    nt = (((1,), (1,)), ((), ()))
    tn = (((0,), (0,)), ((), ()))
    bs = range(q_ref.shape[0])

    def dot(a, b, dims=None):
        if dims is None:
            return jnp.dot(a, b, preferred_element_type=F32)
        return lax.dot_general(a, b, dims, preferred_element_type=F32)

    tri = tri_ref[...]
    tri_b = trib_ref[...]
    gkb = gkb_ref[...]
    nw = nw_ref[...]
    w_hi, w_lo, _ = _split3(gkw_ref[...])
    sm = [_split3(sm_ref[b]) for b in bs]
    pre = [dot(s[0], w_hi) + dot(s[0], w_lo) + dot(s[1], w_hi) + gkb for s in sm]
    gk = [-_softplus(-p) * (1.0 / GLA_GATE_NORMALIZER) for p in pre]
    gcs = [sum(dot(tri_b, part) for part in _split3(g)) for g in gk]
    kf = [k_ref[b].astype(F32) for b in bs]
    vb = [v_ref[b] for b in bs]
    qg = [(q_ref[b].astype(F32) * scale * jnp.exp(gcs[b])).astype(BF16) for b in bs]
    kg = [(kf[b] * jnp.exp(-gcs[b])).astype(BF16) for b in bs]
    att = [(dot(qg[b], kg[b], nt) * tri).astype(BF16) for b in bs]
    o_intra = [dot(att[b], vb[b]) for b in bs]

    st = [st_ref[b] for b in bs]
    outs = [[] for _ in bs]
    for cc in range(chunks_per_step):
        rs = slice(cc * c, (cc + 1) * c)
        for b in bs:
            g_last = gcs[b][(cc + 1) * c - 1:(cc + 1) * c, :]
            kd = (kf[b][rs] * jnp.exp(g_last - gcs[b][rs])).astype(BF16)
            outs[b].append(o_intra[b][rs] + dot(qg[b][rs], st[b].astype(BF16), nt))
            st[b] = st[b] * jnp.exp(g_last) + dot(vb[b][rs], kd, tn)
    for b in bs:
        st_ref[b] = st[b]
    for b in bs:
        o = jnp.concatenate(outs[b], axis=0)
        o = o * lax.rsqrt(jnp.mean(o * o, axis=-1, keepdims=True) + RMS_EPS) * nw
        o_ref[b] = (o * _silu(g_ref[b].astype(F32))).astype(o_ref.dtype)


def _gla(proj, small, gkw_pad, gkb, nw, *, batch, seq, chunks_per_step=4):
    rows = GLA_CHUNK * chunks_per_step
    steps = seq // rows
    hk, hv = GLA_HEAD_K, GLA_HEAD_V
    tri = _gla_cumsum_matrix(rows)

    proj3 = proj.reshape(batch, seq, N_MAIN)
    small3 = small.reshape(batch, seq, LANE)
    out = pl.pallas_call(
        functools.partial(_gla_kernel, chunks_per_step=chunks_per_step),
        grid=(GLA_N_HEADS, steps),
        in_specs=[
            pl.BlockSpec((batch, rows, hk), lambda h, c: (0, c, OFF_Q // hk + h)),
            pl.BlockSpec((batch, rows, hk), lambda h, c: (0, c, OFF_K // hk + h)),
            pl.BlockSpec((batch, rows, hv), lambda h, c: (0, c, OFF_V // hv + h)),
            pl.BlockSpec((batch, rows, hv), lambda h, c: (0, c, OFF_G // hv + h)),
            pl.BlockSpec((batch, rows, LANE), lambda h, c: (0, c, 0)),
            pl.BlockSpec((LANE, hk), lambda h, c: (0, h)),
            pl.BlockSpec((1, hk), lambda h, c: (0, h)),
            pl.BlockSpec((1, hv), lambda h, c: (0, 0)),
            pl.BlockSpec((rows, rows), lambda h, c: (0, 0)),
            pl.BlockSpec((rows, rows), lambda h, c: (0, 0)),
        ],
        out_specs=pl.BlockSpec((batch, rows, hv), lambda h, c: (0, c, h)),
        out_shape=jax.ShapeDtypeStruct((batch, seq, GLA_D_VALUE), BF16),
        scratch_shapes=[pltpu.VMEM((batch, hv, hk), F32)],
        compiler_params=_cparams(2),
    )(proj3, proj3, proj3, proj3, small3, gkw_pad, gkb, nw, tri, tri.astype(BF16))
    return out.reshape(batch * seq, GLA_D_VALUE)


def _small_in_proj_weight(w_in_t):
    c0 = OFF_Q
    c1 = c0 + SSD_N_HEADS
    c2 = c1 + 2 * GLA_D_KEY + 2 * GLA_D_VALUE
    c3 = c2 + GLA_GATE_RANK
    layers, _, k = w_in_t.shape
    pad = jnp.zeros((layers, LANE - SSD_N_HEADS - GLA_GATE_RANK, k), w_in_t.dtype)
    return jnp.concatenate([w_in_t[:, c0:c1, :], w_in_t[:, c2:c3, :], pad], axis=1)


def _head_rows(v):
    vg = v.reshape(SSD_N_GROUPS, SSD_HEADS_PER_GROUP, 1)
    return jnp.broadcast_to(vg, (SSD_N_GROUPS, SSD_HEADS_PER_GROUP, LANE))


def _layer(layer, xf, xb, batch, seq, w_in, w_small, conv_w, conv_b, dt_bias, a_log, d_skip, ssd_norm_w,
           gk_w, gk_b, gla_norm_w, w_ssd, w_gla, gate_bias, w_out_b, ln1_g, ln1_b,
           w_up, w_down_b, ln2_g, ln2_b):
    proj = _in_proj(xb, w_in, layer)
    small, small_t = _small_proj(xb, w_small, layer)

    gw, n = SSD_GROUP_W, SSD_D_STATE
    ng = SSD_N_GROUPS

    def conv_cols(v):
        lead = v.shape[:-1]
        xs = v[..., :SSD_D_INNER].reshape(lead + (ng, gw))
        bs = v[..., SSD_D_INNER:SSD_D_INNER + ng * n].reshape(lead + (ng, n))
        cs = v[..., SSD_D_INNER + ng * n:].reshape(lead + (ng, n))
        return jnp.moveaxis(jnp.concatenate([xs, bs, cs], axis=-1), -2, 0)

    cw_g = conv_cols(conv_w)
    cb_g = conv_cols(conv_b[None, :])
    dtb_g = _head_rows(dt_bias)
    alog_g = _head_rows(a_log)
    dexp_g = jnp.repeat(d_skip, SSD_HEAD_DIM).reshape(ng, 1, gw)
    nw_g = ssd_norm_w.reshape(ng, 1, gw)
    y = _ssd(proj, small_t, cw_g, cb_g, dtb_g, alog_g, dexp_g, nw_g, batch=batch, seq=seq)

    gkw_pad = jnp.zeros((LANE, GLA_D_KEY), F32).at[SMALL_GK:SMALL_GK + GLA_GATE_RANK].set(gk_w)
    o = _gla(proj, small, gkw_pad, gk_b[None, :], gla_norm_w[None, :], batch=batch, seq=seq)

    merged = _merge(y, o, proj, w_ssd, w_gla, gate_bias, layer)
    hf, hb = _proj_ln(merged, w_out_b, xf, ln1_g, ln1_b, layer)
    u = _matmul(hb, w_up, layer, tm=1024, tn=1024, out_dtype=BF16, act="relu2")
    return _down_ln(u, w_down_b, hf, ln2_g, ln2_b, layer)


def kernel(x, w_in, ssd_conv_w, ssd_conv_b, ssd_dt_bias, ssd_A_log, ssd_D, ssd_norm_w, gla_gk_w, gla_gk_b, gla_norm_w, w_ssd_branch, w_gla_branch, gate_bias, w_out, ln1_g, ln1_b, w_up, w_down, ln2_g, ln2_b):
    batch, seq, d = x.shape
    xf = x.reshape(batch * seq, d)
    xb = xf.astype(BF16)
    w_out_b = w_out.astype(BF16)
    w_down_b = w_down.astype(BF16)
    ln1_g, ln1_b, ln2_g, ln2_b = (v[:, None, :] for v in (ln1_g, ln1_b, ln2_g, ln2_b))
    w_in_t = jnp.swapaxes(w_in, 1, 2)
    w_small = _small_in_proj_weight(w_in_t)
    for l in range(w_in.shape[0]):
        xf, xb = _layer(l, xf, xb, batch, seq, w_in_t, w_small, ssd_conv_w[l], ssd_conv_b[l], ssd_dt_bias[l],
                        ssd_A_log[l], ssd_D[l], ssd_norm_w[l], gla_gk_w[l], gla_gk_b[l],
                        gla_norm_w[l], w_ssd_branch, w_gla_branch, gate_bias, w_out_b,
                        ln1_g, ln1_b, w_up, w_down_b, ln2_g, ln2_b)
    return xf.reshape(batch, seq, d)
```

```python
import functools

import jax
import jax.numpy as jnp
from jax import lax
from jax.experimental import pallas as pl
from jax.experimental.pallas import tpu as pltpu

F32 = jnp.float32
BF16 = jnp.bfloat16
HIGHEST = lax.Precision.HIGHEST

D_MODEL = 2048
DEPTH = 4
ALPHA = (2.0 * DEPTH) ** 0.25
LN_EPS = 1e-5
RMS_EPS = 1e-5

SSD_D_INNER = 2 * D_MODEL
SSD_HEAD_DIM = 64
SSD_N_HEADS = SSD_D_INNER // SSD_HEAD_DIM
SSD_N_GROUPS = 8
SSD_HEADS_PER_GROUP = SSD_N_HEADS // SSD_N_GROUPS
SSD_D_STATE = 128
SSD_CONV_W = 4
SSD_CHUNK = 128
SSD_GROUP_W = SSD_HEADS_PER_GROUP * SSD_HEAD_DIM
SSD_GROUP_CONV_W = SSD_GROUP_W + 2 * SSD_D_STATE

GLA_N_HEADS = 4
GLA_D_KEY = D_MODEL // 2
GLA_D_VALUE = D_MODEL
GLA_HEAD_K = GLA_D_KEY // GLA_N_HEADS
GLA_HEAD_V = GLA_D_VALUE // GLA_N_HEADS
GLA_GATE_RANK = 16
GLA_GATE_NORMALIZER = 16.0
GLA_CHUNK = 64

D_FF = 4 * D_MODEL

OFF_Z = 0
OFF_X = OFF_Z + SSD_D_INNER
OFF_B = OFF_X + SSD_D_INNER
OFF_C = OFF_B + SSD_N_GROUPS * SSD_D_STATE
OFF_Q = OFF_C + SSD_N_GROUPS * SSD_D_STATE
OFF_K = OFF_Q + GLA_D_KEY
OFF_V = OFF_K + GLA_D_KEY
OFF_G = OFF_V + GLA_D_VALUE
OFF_GS = OFF_G + GLA_D_VALUE
OFF_GG = OFF_GS + D_MODEL
N_MAIN = OFF_GG + D_MODEL
LANE = 128
SMALL_DT = 0
SMALL_GK = SSD_N_HEADS

VMEM_LIMIT = 56 * 1024 * 1024


def _cparams(n_axes):
    return pltpu.CompilerParams(dimension_semantics=("arbitrary",) * n_axes,
                                vmem_limit_bytes=VMEM_LIMIT)


def _sigmoid(x):
    return 0.5 + 0.5 * jnp.tanh(0.5 * x)


def _silu(x):
    h = 0.5 * x
    return h + h * jnp.tanh(h)


def _softplus(x):
    return jnp.maximum(x, 0.0) + jnp.log(1.0 + jnp.exp(-jnp.abs(x)))


def _layer_norm(v, g, b):
    mu = jnp.mean(v, axis=-1, keepdims=True)
    vc = v - mu
    var = jnp.mean(vc * vc, axis=-1, keepdims=True)
    return vc * lax.rsqrt(var + LN_EPS) * g + b


CAST_ROWS = 256


def _cast_weight(dst_ref, src_ref):
    def body(r, carry):
        rows = pl.ds(pl.multiple_of(r * CAST_ROWS, CAST_ROWS), CAST_ROWS)
        dst_ref[rows, :] = src_ref[rows, :].astype(BF16)
        return carry
    lax.fori_loop(0, src_ref.shape[0] // CAST_ROWS, body, 0)


def _mm_kernel(a_ref, w_ref, o_ref, wt_ref, *, act):
    @pl.when(pl.program_id(1) == 0)
    def _():
        _cast_weight(wt_ref, w_ref)

    acc = jnp.dot(a_ref[...], wt_ref[...], preferred_element_type=F32)
    if act == "relu2":
        r = jnp.maximum(acc, 0.0)
        acc = r * r
    o_ref[...] = acc.astype(o_ref.dtype)


def _matmul(a, w, layer, *, tm, tn, out_dtype, act=None):
    m, k = a.shape
    n = w.shape[2]
    return pl.pallas_call(
        functools.partial(_mm_kernel, act=act),
        grid=(n // tn, m // tm),
        in_specs=[pl.BlockSpec((tm, k), lambda j, i: (i, 0)),
                  pl.BlockSpec((None, k, tn), lambda j, i: (layer, 0, j))],
        out_specs=pl.BlockSpec((tm, tn), lambda j, i: (i, j)),
        out_shape=jax.ShapeDtypeStruct((m, n), out_dtype),
        scratch_shapes=[pltpu.VMEM((k, tn), BF16)],
        compiler_params=_cparams(2),
    )(a, w)


IN_SHIFT_QKVG = SSD_N_HEADS
IN_SHIFT_GATES = SSD_N_HEADS + GLA_GATE_RANK


def _in_proj_kernel(a_ref, w_ref, o_ref, wt_ref):
    @pl.when(pl.program_id(1) == 0)
    def _():
        _cast_weight(wt_ref, w_ref.at[0])

    o_ref[...] = lax.dot_general(a_ref[...], wt_ref[...], (((1,), (1,)), ((), ())),
                                 preferred_element_type=F32).astype(o_ref.dtype)


def _in_proj(a, w_in_t, layer, *, tm=2048, tn=1024):
    m, k = a.shape
    assert OFF_Q % tn == 0 and OFF_GS % tn == 0 and N_MAIN % tn == 0

    def w_rows(j, i):
        past_dt = (j >= OFF_Q // tn).astype(jnp.int32)
        past_gk = (j >= OFF_GS // tn).astype(jnp.int32)
        shift = past_dt * IN_SHIFT_QKVG + past_gk * (IN_SHIFT_GATES - IN_SHIFT_QKVG)
        return layer, pl.multiple_of(j * tn + shift, 16), 0

    return pl.pallas_call(
        _in_proj_kernel,
        grid=(N_MAIN // tn, m // tm),
        in_specs=[pl.BlockSpec((tm, k), lambda j, i: (i, 0)),
                  pl.BlockSpec((pl.Element(1), pl.Element(tn), pl.Element(k)), w_rows)],
        out_specs=pl.BlockSpec((tm, tn), lambda j, i: (i, j)),
        out_shape=jax.ShapeDtypeStruct((m, N_MAIN), BF16),
        scratch_shapes=[pltpu.VMEM((tn, k), BF16)],
        compiler_params=_cparams(2),
    )(a, w_in_t)


def _small_proj_kernel(a_ref, b_ref, o_ref, ot_ref):
    acc = lax.dot_general(a_ref[...], b_ref[...].astype(BF16), (((1,), (1,)), ((), ())),
                          preferred_element_type=F32)
    o_ref[...] = acc
    ot_ref[...] = acc.T


def _small_proj(a, b_t, layer, *, tm=1024):
    m, k = a.shape
    return pl.pallas_call(
        _small_proj_kernel,
        grid=(m // tm,),
        in_specs=[pl.BlockSpec((tm, k), lambda i: (i, 0)),
                  pl.BlockSpec((None, LANE, k), lambda i: (layer, 0, 0))],
        out_specs=[pl.BlockSpec((tm, LANE), lambda i: (i, 0)),
                   pl.BlockSpec((LANE, tm), lambda i: (0, i))],
        out_shape=[jax.ShapeDtypeStruct((m, LANE), F32),
                   jax.ShapeDtypeStruct((LANE, m), F32)],
        compiler_params=_cparams(1),
    )(a, b_t)


def _merge_kernel(y_ref, o_ref, ws_ref, wg_ref, gs_ref, gg_ref, gb_ref, out_ref, wst_ref, wgt_ref):
    @pl.when(pl.program_id(1) == 0)
    def _():
        _cast_weight(wst_ref, ws_ref)
        _cast_weight(wgt_ref, wg_ref)

    bs = jnp.dot(y_ref[...], wst_ref[...], preferred_element_type=F32)
    bg = jnp.dot(o_ref[...], wgt_ref[...], preferred_element_type=F32)
    gb = gb_ref[...]
    s_gate = _sigmoid(gs_ref[...].astype(F32) + gb[0:1, :])
    g_gate = _sigmoid(gg_ref[...].astype(F32) + gb[1:2, :])
    out_ref[...] = (s_gate * bs + g_gate * bg).astype(out_ref.dtype)


def _merge(y, o, proj, w_ssd, w_gla, gate_bias, layer, *, tm=512, tn=512):
    t = y.shape[0]
    return pl.pallas_call(
        _merge_kernel,
        grid=(D_MODEL // tn, t // tm),
        in_specs=[pl.BlockSpec((tm, SSD_D_INNER), lambda j, i: (i, 0)),
                  pl.BlockSpec((tm, GLA_D_VALUE), lambda j, i: (i, 0)),
                  pl.BlockSpec((None, SSD_D_INNER, tn), lambda j, i: (layer, 0, j)),
                  pl.BlockSpec((None, GLA_D_VALUE, tn), lambda j, i: (layer, 0, j)),
                  pl.BlockSpec((tm, tn), lambda j, i: (i, OFF_GS // tn + j)),
                  pl.BlockSpec((tm, tn), lambda j, i: (i, OFF_GG // tn + j)),
                  pl.BlockSpec((None, 2, tn), lambda j, i: (layer, 0, j))],
        out_specs=pl.BlockSpec((tm, tn), lambda j, i: (i, j)),
        out_shape=jax.ShapeDtypeStruct((t, D_MODEL), BF16),
        scratch_shapes=[pltpu.VMEM((SSD_D_INNER, tn), BF16), pltpu.VMEM((GLA_D_VALUE, tn), BF16)],
        compiler_params=_cparams(2),
    )(y, o, w_ssd, w_gla, proj, proj, gate_bias)


def _proj_ln_kernel(a_ref, w_ref, res_ref, g_ref, b_ref, of_ref, ob_ref):
    acc = jnp.dot(a_ref[...], w_ref[...], preferred_element_type=F32)
    out = _layer_norm(ALPHA * res_ref[...] + acc, g_ref[...], b_ref[...])
    of_ref[...] = out
    ob_ref[...] = out.astype(BF16)


def _proj_ln(a, w, res, g, b, layer, *, tm=512):
    t, k = a.shape
    return pl.pallas_call(
        _proj_ln_kernel,
        grid=(t // tm,),
        in_specs=[pl.BlockSpec((tm, k), lambda i: (i, 0)),
                  pl.BlockSpec((None, k, D_MODEL), lambda i: (layer, 0, 0)),
                  pl.BlockSpec((tm, D_MODEL), lambda i: (i, 0)),
                  pl.BlockSpec((None, 1, D_MODEL), lambda i: (layer, 0, 0)),
                  pl.BlockSpec((None, 1, D_MODEL), lambda i: (layer, 0, 0))],
        out_specs=[pl.BlockSpec((tm, D_MODEL), lambda i: (i, 0)),
                   pl.BlockSpec((tm, D_MODEL), lambda i: (i, 0))],
        out_shape=[jax.ShapeDtypeStruct((t, D_MODEL), F32),
                   jax.ShapeDtypeStruct((t, D_MODEL), BF16)],
        compiler_params=_cparams(1),
    )(a, w, res, g, b)


def _down_ln_kernel(u_ref, w_ref, res_ref, g_ref, b_ref, of_ref, ob_ref, acc_ref):
    kk = pl.program_id(1)

    @pl.when(kk == 0)
    def _():
        acc_ref[...] = jnp.zeros_like(acc_ref)

    acc_ref[...] += jnp.dot(u_ref[...], w_ref[...], preferred_element_type=F32)

    @pl.when(kk == pl.num_programs(1) - 1)
    def _():
        out = _layer_norm(ALPHA * res_ref[...] + acc_ref[...], g_ref[...], b_ref[...])
        of_ref[...] = out
        ob_ref[...] = out.astype(BF16)


def _down_ln(u, w, res, g, b, layer, *, tm=512, tk=2048):
    t, k = u.shape
    return pl.pallas_call(
        _down_ln_kernel,
        grid=(t // tm, k // tk),
        in_specs=[pl.BlockSpec((tm, tk), lambda i, kk: (i, kk)),
                  pl.BlockSpec((None, tk, D_MODEL), lambda i, kk: (layer, kk, 0)),
                  pl.BlockSpec((tm, D_MODEL), lambda i, kk: (i, 0)),
                  pl.BlockSpec((None, 1, D_MODEL), lambda i, kk: (layer, 0, 0)),
                  pl.BlockSpec((None, 1, D_MODEL), lambda i, kk: (layer, 0, 0))],
        out_specs=[pl.BlockSpec((tm, D_MODEL), lambda i, kk: (i, 0)),
                   pl.BlockSpec((tm, D_MODEL), lambda i, kk: (i, 0))],
        out_shape=[jax.ShapeDtypeStruct((t, D_MODEL), F32),
                   jax.ShapeDtypeStruct((t, D_MODEL), BF16)],
        scratch_shapes=[pltpu.VMEM((tm, D_MODEL), F32)],
        compiler_params=_cparams(2),
    )(u, w, res, g, b)


_EXP_SKIP = 1
_EXP_SECTIONS = 4


def _ssd_expand_matrix():
    r = lax.broadcasted_iota(jnp.int32, (LANE, _EXP_SECTIONS * SSD_GROUP_W), 0)
    c = lax.broadcasted_iota(jnp.int32, (LANE, _EXP_SECTIONS * SSD_GROUP_W), 1)
    src = (c // SSD_GROUP_W + _EXP_SKIP) * SSD_HEADS_PER_GROUP + (c % SSD_GROUP_W) // SSD_HEAD_DIM
    e = (r == src).astype(BF16)
    return jnp.concatenate([e, e], axis=0)


def _ssd_shift_matrix():
    q = SSD_CHUNK
    r = lax.broadcasted_iota(jnp.int32, (3 * q, 2 * q), 0)
    c = lax.broadcasted_iota(jnp.int32, (3 * q, 2 * q), 1)
    return (c == q + r % q - (r // q + 1)).astype(BF16)


def _split3(v):
    hi = v.astype(BF16)
    r1 = v - hi.astype(F32)
    mid = r1.astype(BF16)
    lo = (r1 - mid.astype(F32)).astype(BF16)
    return hi, mid, lo


def _ssd_kernel(x_ref, b_ref, c_ref, z_ref, dt_ref, cw_ref, cb_ref, dtb_ref, alog_ref,
                dexp_ref, nw_ref, e_ref, shift_ref, y_ref, tail_ref, st_ref, *, chunks_per_step):
    q = SSD_CHUNK
    hg = SSD_HEADS_PER_GROUP
    p = SSD_HEAD_DIM
    gw = SSD_GROUP_W
    n = SSD_D_STATE

    @pl.when(pl.program_id(2) == 0)
    def _():
        tail_ref[...] = jnp.zeros_like(tail_ref)
        st_ref[...] = jnp.zeros_like(st_ref)

    row = lax.broadcasted_iota(jnp.int32, (q, q), 0)
    col = lax.broadcasted_iota(jnp.int32, (q, q), 1)
    causal = col <= row
    tri_b = causal.astype(BF16)
    even_lanes = col < p
    nt = (((1,), (1,)), ((), ()))

    cw = cw_ref[...]
    cbias = cb_ref[...]
    dtb = dtb_ref[...]
    a_neg = -jnp.exp(alog_ref[...])
    dexp = dexp_ref[...]
    nw = nw_ref[...]
    shift = shift_ref[...]

    def dot(a, b, dims=None):
        if dims is None:
            return jnp.dot(a, b, preferred_element_type=F32)
        return lax.dot_general(a, b, dims, preferred_element_type=F32)

    cs_ = range(chunks_per_step)
    rows = [pl.ds(cc * q, q) for cc in cs_]

    raw = [jnp.concatenate([x_ref[r, :], b_ref[r, :], c_ref[r, :]], axis=1) for r in rows]
    prev = [tail_ref[...]] + raw[:-1]
    sh = [dot(shift, jnp.concatenate([prev[cc], raw[cc]], axis=0)) for cc in cs_]
    xbc = []
    for cc in cs_:
        acc = cbias + cw[3:4, :] * raw[cc].astype(F32)
        for s in range(1, SSD_CONV_W):
            acc = acc + cw[3 - s:4 - s, :] * sh[cc][(s - 1) * q:s * q, :]
        xbc.append(_silu(acc))
    xs = [v[:, 0:gw] for v in xbc]
    bm_b = [v[:, gw:gw + n].astype(BF16) for v in xbc]
    cm_b = [v[:, gw + n:].astype(BF16) for v in xbc]

    dt_t = [_softplus(dt_ref[:, cc * q:(cc + 1) * q] + dtb) for cc in cs_]
    a_cs_t = [sum(dot(part, tri_b, nt) for part in _split3(v * a_neg))
              for v in dt_t]
    cs = []
    for cc in cs_:
        a_last = a_cs_t[cc][:, q - 1:q]
        stack = jnp.concatenate(
            [a_cs_t[cc], dt_t[cc], jnp.exp(a_cs_t[cc]), jnp.exp(a_last - a_cs_t[cc]),
             jnp.broadcast_to(jnp.exp(a_last), (hg, q)),
             jnp.zeros((q - (_EXP_SECTIONS + _EXP_SKIP) * hg, q), F32)], axis=0)
        cs.append(stack.T)
    hilo = []
    for v in cs:
        hi = v.astype(BF16)
        hilo.append(jnp.concatenate([hi, (v - hi.astype(F32)).astype(BF16)], axis=1))
    ex = [dot(v, e_ref[:, 0:3 * gw]) for v in hilo]
    el_exp = [dot(v[0:hg, :], e_ref[:, 3 * gw:4 * gw])[0:1, :] for v in hilo]

    xdt = [xs[cc] * ex[cc][:, 0:gw] for cc in cs_]
    xdt_b = [v.astype(BF16) for v in xdt]
    xw_b = [(xdt[cc] * ex[cc][:, 2 * gw:3 * gw]).astype(BF16) for cc in cs_]
    cbm = [jnp.where(causal, dot(cm_b[cc], bm_b[cc], nt), 0.0) for cc in cs_]
    states = [dot(bm_b[cc], xw_b[cc], (((0,), (0,)), ((), ()))) for cc in cs_]

    y_diag = []
    for cc in cs_:
        ys = []
        for hp in range(hg // 2):
            l_mats = []
            for hh in (2 * hp, 2 * hp + 1):
                seg = cs[cc][:, hh:hh + 1] - a_cs_t[cc][hh:hh + 1, :]
                l_mats.append((cbm[cc] * jnp.exp(jnp.minimum(seg, 0.0))).astype(BF16))
            x_pair = xdt_b[cc][:, 2 * hp * p:(2 * hp + 2) * p]
            zero = jnp.zeros_like(x_pair)
            rhs = jnp.concatenate([jnp.where(even_lanes, x_pair, zero),
                                   jnp.where(even_lanes, zero, x_pair)], axis=0)
            ys.append(dot(jnp.concatenate(l_mats, axis=1), rhs))
        y_diag.append(jnp.concatenate(ys, axis=1))

    st = st_ref[...]
    y_off = []
    for cc in cs_:
        y_off.append(dot(cm_b[cc], st.astype(BF16)) * ex[cc][:, gw:2 * gw])
        st = st * el_exp[cc] + states[cc]
    st_ref[...] = st
    tail_ref[...] = raw[-1]

    for cc in cs_:
        y = y_diag[cc] + y_off[cc] + dexp * xs[cc]
        y = y * _silu(z_ref[rows[cc], :].astype(F32))
        y = y * lax.rsqrt(jnp.mean(y * y, axis=-1, keepdims=True) + RMS_EPS) * nw
        y_ref[rows[cc], :] = y.astype(y_ref.dtype)


def _ssd(proj, small_t, cw_g, cb_g, dtb_g, alog_g, dexp_g, nw_g, *, batch, seq, chunks_per_step=8):
    rows = SSD_CHUNK * chunks_per_step
    steps = seq // rows
    gw = SSD_GROUP_W
    n = SSD_D_STATE
    hg = SSD_HEADS_PER_GROUP
    e_mat = _ssd_expand_matrix()
    shift = _ssd_shift_matrix()

    def row_blk(b, c):
        return b * steps + c

    return pl.pallas_call(
        functools.partial(_ssd_kernel, chunks_per_step=chunks_per_step),
        grid=(batch, SSD_N_GROUPS, steps),
        in_specs=[
            pl.BlockSpec((rows, gw), lambda b, g, c: (row_blk(b, c), OFF_X // gw + g)),
            pl.BlockSpec((rows, n), lambda b, g, c: (row_blk(b, c), OFF_B // n + g)),
            pl.BlockSpec((rows, n), lambda b, g, c: (row_blk(b, c), OFF_C // n + g)),
            pl.BlockSpec((rows, gw), lambda b, g, c: (row_blk(b, c), OFF_Z // gw + g)),
            pl.BlockSpec((hg, rows), lambda b, g, c: (g, row_blk(b, c))),
            pl.BlockSpec((None, SSD_CONV_W, SSD_GROUP_CONV_W), lambda b, g, c: (g, 0, 0)),
            pl.BlockSpec((None, 1, SSD_GROUP_CONV_W), lambda b, g, c: (g, 0, 0)),
            pl.BlockSpec((None, hg, LANE), lambda b, g, c: (g, 0, 0)),
            pl.BlockSpec((None, hg, LANE), lambda b, g, c: (g, 0, 0)),
            pl.BlockSpec((None, 1, gw), lambda b, g, c: (g, 0, 0)),
            pl.BlockSpec((None, 1, gw), lambda b, g, c: (g, 0, 0)),
            pl.BlockSpec(e_mat.shape, lambda b, g, c: (0, 0)),
            pl.BlockSpec(shift.shape, lambda b, g, c: (0, 0)),
        ],
        out_specs=pl.BlockSpec((rows, gw), lambda b, g, c: (row_blk(b, c), g)),
        out_shape=jax.ShapeDtypeStruct((batch * seq, SSD_D_INNER), BF16),
        scratch_shapes=[pltpu.VMEM((SSD_CHUNK, SSD_GROUP_CONV_W), BF16),
                        pltpu.VMEM((SSD_D_STATE, gw), F32)],
        compiler_params=_cparams(3),
    )(proj, proj, proj, proj, small_t, cw_g, cb_g, dtb_g, alog_g, dexp_g, nw_g, e_mat, shift)


def _gla_cumsum_matrix(rows):
    r = lax.broadcasted_iota(jnp.int32, (rows, rows), 0)
    c = lax.broadcasted_iota(jnp.int32, (rows, rows), 1)
    return jnp.logical_and(r // GLA_CHUNK == c // GLA_CHUNK, c <= r).astype(F32)


def _gla_kernel(q_ref, k_ref, v_ref, g_ref, sm_ref, gkw_ref, gkb_ref, nw_ref, tri_ref, trib_ref,
                o_ref, st_ref, *, chunks_per_step):
    @pl.when(pl.program_id(1) == 0)
    def _():
        st_ref[...] = jnp.zeros_like(st_ref)

    c = GLA_CHUNK
    scale = GLA_HEAD_K ** -0.5
    nt = (((1,), (1,)), ((), ()))
    tn = (((0,), (0,)), ((), ()))
    bs = range(q_ref.shape[0])

    def dot(a, b, dims=None):
        if dims is None:
            return jnp.dot(a, b, preferred_element_type=F32)
        return lax.dot_general(a, b, dims, preferred_element_type=F32)

    tri = tri_ref[...]
    tri_b = trib_ref[...]
    gkb = gkb_ref[...]
    nw = nw_ref[...]
    w_hi, w_lo, _ = _split3(gkw_ref[...])
    sm = [_split3(sm_ref[b]) for b in bs]
    pre = [dot(s[0], w_hi) + dot(s[0], w_lo) + dot(s[1], w_hi) + gkb for s in sm]
    gk = [-_softplus(-p) * (1.0 / GLA_GATE_NORMALIZER) for p in pre]
    gcs = [sum(dot(tri_b, part) for part in _split3(g)) for g in gk]
    kf = [k_ref[b].astype(F32) for b in bs]
    vb = [v_ref[b] for b in bs]
    qg = [(q_ref[b].astype(F32) * scale * jnp.exp(gcs[b])).astype(BF16) for b in bs]
    kg = [(kf[b] * jnp.exp(-gcs[b])).astype(BF16) for b in bs]
    att = [(dot(qg[b], kg[b], nt) * tri).astype(BF16) for b in bs]
    o_intra = [dot(att[b], vb[b]) for b in bs]

    st = [st_ref[b] for b in bs]
    outs = [[] for _ in bs]
    for cc in range(chunks_per_step):
        rs = slice(cc * c, (cc + 1) * c)
        for b in bs:
            g_last = gcs[b][(cc + 1) * c - 1:(cc + 1) * c, :]
            kd = (kf[b][rs] * jnp.exp(g_last - gcs[b][rs])).astype(BF16)
            outs[b].append(o_intra[b][rs] + dot(qg[b][rs], st[b].astype(BF16), nt))
            st[b] = st[b] * jnp.exp(g_last) + dot(vb[b][rs], kd, tn)
    for b in bs:
        st_ref[b] = st[b]
    for b in bs:
        o = jnp.concatenate(outs[b], axis=0)
        o = o * lax.rsqrt(jnp.mean(o * o, axis=-1, keepdims=True) + RMS_EPS) * nw
        o_ref[b] = (o * _silu(g_ref[b].astype(F32))).astype(o_ref.dtype)


def _gla(proj, small, gkw_pad, gkb, nw, *, batch, seq, chunks_per_step=4):
    rows = GLA_CHUNK * chunks_per_step
    steps = seq // rows
    hk, hv = GLA_HEAD_K, GLA_HEAD_V
    tri = _gla_cumsum_matrix(rows)

    proj3 = proj.reshape(batch, seq, N_MAIN)
    small3 = small.reshape(batch, seq, LANE)
    out = pl.pallas_call(
        functools.partial(_gla_kernel, chunks_per_step=chunks_per_step),
        grid=(GLA_N_HEADS, steps),
        in_specs=[
            pl.BlockSpec((batch, rows, hk), lambda h, c: (0, c, OFF_Q // hk + h)),
            pl.BlockSpec((batch, rows, hk), lambda h, c: (0, c, OFF_K // hk + h)),
            pl.BlockSpec((batch, rows, hv), lambda h, c: (0, c, OFF_V // hv + h)),
            pl.BlockSpec((batch, rows, hv), lambda h, c: (0, c, OFF_G // hv + h)),
            pl.BlockSpec((batch, rows, LANE), lambda h, c: (0, c, 0)),
            pl.BlockSpec((LANE, hk), lambda h, c: (0, h)),
            pl.BlockSpec((1, hk), lambda h, c: (0, h)),
            pl.BlockSpec((1, hv), lambda h, c: (0, 0)),
            pl.BlockSpec((rows, rows), lambda h, c: (0, 0)),
            pl.BlockSpec((rows, rows), lambda h, c: (0, 0)),
        ],
        out_specs=pl.BlockSpec((batch, rows, hv), lambda h, c: (0, c, h)),
        out_shape=jax.ShapeDtypeStruct((batch, seq, GLA_D_VALUE), BF16),
        scratch_shapes=[pltpu.VMEM((batch, hv, hk), F32)],
        compiler_params=_cparams(2),
    )(proj3, proj3, proj3, proj3, small3, gkw_pad, gkb, nw, tri, tri.astype(BF16))
    return out.reshape(batch * seq, GLA_D_VALUE)


def _small_in_proj_weight(w_in_t):
    c0 = OFF_Q
    c1 = c0 + SSD_N_HEADS
    c2 = c1 + 2 * GLA_D_KEY + 2 * GLA_D_VALUE
    c3 = c2 + GLA_GATE_RANK
    layers, _, k = w_in_t.shape
    pad = jnp.zeros((layers, LANE - SSD_N_HEADS - GLA_GATE_RANK, k), w_in_t.dtype)
    return jnp.concatenate([w_in_t[:, c0:c1, :], w_in_t[:, c2:c3, :], pad], axis=1)


def _head_rows(v):
    vg = v.reshape(SSD_N_GROUPS, SSD_HEADS_PER_GROUP, 1)
    return jnp.broadcast_to(vg, (SSD_N_GROUPS, SSD_HEADS_PER_GROUP, LANE))


def _layer(layer, xf, xb, batch, seq, w_in, w_small, conv_w, conv_b, dt_bias, a_log, d_skip, ssd_norm_w,
           gk_w, gk_b, gla_norm_w, w_ssd, w_gla, gate_bias, w_out_b, ln1_g, ln1_b,
           w_up, w_down_b, ln2_g, ln2_b):
    proj = _in_proj(xb, w_in, layer)
    small, small_t = _small_proj(xb, w_small, layer)

    gw, n = SSD_GROUP_W, SSD_D_STATE
    ng = SSD_N_GROUPS

    def conv_cols(v):
        lead = v.shape[:-1]
        xs = v[..., :SSD_D_INNER].reshape(lead + (ng, gw))
        bs = v[..., SSD_D_INNER:SSD_D_INNER + ng * n].reshape(lead + (ng, n))
        cs = v[..., SSD_D_INNER + ng * n:].reshape(lead + (ng, n))
        return jnp.moveaxis(jnp.concatenate([xs, bs, cs], axis=-1), -2, 0)

    cw_g = conv_cols(conv_w)
    cb_g = conv_cols(conv_b[None, :])
    dtb_g = _head_rows(dt_bias)
    alog_g = _head_rows(a_log)
    dexp_g = jnp.repeat(d_skip, SSD_HEAD_DIM).reshape(ng, 1, gw)
    nw_g = ssd_norm_w.reshape(ng, 1, gw)
    y = _ssd(proj, small_t, cw_g, cb_g, dtb_g, alog_g, dexp_g, nw_g, batch=batch, seq=seq)

    gkw_pad = jnp.zeros((LANE, GLA_D_KEY), F32).at[SMALL_GK:SMALL_GK + GLA_GATE_RANK].set(gk_w)
    o = _gla(proj, small, gkw_pad, gk_b[None, :], gla_norm_w[None, :], batch=batch, seq=seq)

    merged = _merge(y, o, proj, w_ssd, w_gla, gate_bias, layer)
    hf, hb = _proj_ln(merged, w_out_b, xf, ln1_g, ln1_b, layer)
    u = _matmul(hb, w_up, layer, tm=1024, tn=1024, out_dtype=BF16, act="relu2")
    return _down_ln(u, w_down_b, hf, ln2_g, ln2_b, layer)


def kernel(x, w_in, ssd_conv_w, ssd_conv_b, ssd_dt_bias, ssd_A_log, ssd_D, ssd_norm_w, gla_gk_w, gla_gk_b, gla_norm_w, w_ssd_branch, w_gla_branch, gate_bias, w_out, ln1_g, ln1_b, w_up, w_down, ln2_g, ln2_b):
    batch, seq, d = x.shape
    xf = x.reshape(batch * seq, d)
    xb = xf.astype(BF16)
    w_out_b = w_out.astype(BF16)
    w_down_b = w_down.astype(BF16)
    ln1_g, ln1_b, ln2_g, ln2_b = (v[:, None, :] for v in (ln1_g, ln1_b, ln2_g, ln2_b))
    w_in_t = jnp.swapaxes(w_in, 1, 2)
    w_small = _small_in_proj_weight(w_in_t)
    for l in range(w_in.shape[0]):
        xf, xb = _layer(l, xf, xb, batch, seq, w_in_t, w_small, ssd_conv_w[l], ssd_conv_b[l], ssd_dt_bias[l],
                        ssd_A_log[l], ssd_D[l], ssd_norm_w[l], gla_gk_w[l], gla_gk_b[l],
                        gla_norm_w[l], w_ssd_branch, w_gla_branch, gate_bias, w_out_b,
                        ln1_g, ln1_b, w_up, w_down_b, ln2_g, ln2_b)
    return xf.reshape(batch, seq, d)
```

```python
import functools

import jax
import jax.numpy as jnp
from jax import lax
from jax.experimental import pallas as pl
from jax.experimental.pallas import tpu as pltpu

F32 = jnp.float32
BF16 = jnp.bfloat16
HIGHEST = lax.Precision.HIGHEST

D_MODEL = 2048
DEPTH = 4
ALPHA = (2.0 * DEPTH) ** 0.25
LN_EPS = 1e-5
RMS_EPS = 1e-5

SSD_D_INNER = 2 * D_MODEL
SSD_HEAD_DIM = 64
SSD_N_HEADS = SSD_D_INNER // SSD_HEAD_DIM
SSD_N_GROUPS = 8
SSD_HEADS_PER_GROUP = SSD_N_HEADS // SSD_N_GROUPS
SSD_D_STATE = 128
SSD_CONV_W = 4
SSD_CHUNK = 128
SSD_GROUP_W = SSD_HEADS_PER_GROUP * SSD_HEAD_DIM
SSD_GROUP_CONV_W = SSD_GROUP_W + 2 * SSD_D_STATE

GLA_N_HEADS = 4
GLA_D_KEY = D_MODEL // 2
GLA_D_VALUE = D_MODEL
GLA_HEAD_K = GLA_D_KEY // GLA_N_HEADS
GLA_HEAD_V = GLA_D_VALUE // GLA_N_HEADS
GLA_GATE_RANK = 16
GLA_GATE_NORMALIZER = 16.0
GLA_CHUNK = 64
GLA_HEADS_PER_STEP = 2

D_FF = 4 * D_MODEL

OFF_Z = 0
OFF_X = OFF_Z + SSD_D_INNER
OFF_B = OFF_X + SSD_D_INNER
OFF_C = OFF_B + SSD_N_GROUPS * SSD_D_STATE
OFF_Q = OFF_C + SSD_N_GROUPS * SSD_D_STATE
OFF_K = OFF_Q + GLA_D_KEY
OFF_V = OFF_K + GLA_D_KEY
OFF_G = OFF_V + GLA_D_VALUE
OFF_GS = OFF_G + GLA_D_VALUE
OFF_GG = OFF_GS + D_MODEL
N_MAIN = OFF_GG + D_MODEL
LANE = 128
SMALL_DT = 0
SMALL_GK = SSD_N_HEADS

VMEM_LIMIT = 56 * 1024 * 1024


def _cparams(n_axes):
    return pltpu.CompilerParams(dimension_semantics=("arbitrary",) * n_axes,
                                vmem_limit_bytes=VMEM_LIMIT)


def _sigmoid(x):
    return 0.5 + 0.5 * jnp.tanh(0.5 * x)


def _silu(x):
    h = 0.5 * x
    return h + h * jnp.tanh(h)


def _softplus(x):
    return jnp.maximum(x, 0.0) + jnp.log(1.0 + jnp.exp(-jnp.abs(x)))


def _layer_norm(v, g, b):
    mu = jnp.mean(v, axis=-1, keepdims=True)
    vc = v - mu
    var = jnp.mean(vc * vc, axis=-1, keepdims=True)
    return vc * lax.rsqrt(var + LN_EPS) * g + b


CAST_ROWS = 256


def _cast_weight(dst_ref, src_ref):
    def body(r, carry):
        rows = pl.ds(pl.multiple_of(r * CAST_ROWS, CAST_ROWS), CAST_ROWS)
        dst_ref[rows, :] = src_ref[rows, :].astype(BF16)
        return carry
    lax.fori_loop(0, src_ref.shape[0] // CAST_ROWS, body, 0)


def _mm_kernel(a_ref, w_ref, o_ref, wt_ref, *, act):
    @pl.when(pl.program_id(1) == 0)
    def _():
        _cast_weight(wt_ref, w_ref)

    acc = jnp.dot(a_ref[...], wt_ref[...], preferred_element_type=F32)
    if act == "relu2":
        r = jnp.maximum(acc, 0.0)
        acc = r * r
    o_ref[...] = acc.astype(o_ref.dtype)


def _matmul(a, w, layer, *, tm, tn, out_dtype, act=None):
    m, k = a.shape
    n = w.shape[2]
    return pl.pallas_call(
        functools.partial(_mm_kernel, act=act),
        grid=(n // tn, m // tm),
        in_specs=[pl.BlockSpec((tm, k), lambda j, i: (i, 0)),
                  pl.BlockSpec((None, k, tn), lambda j, i: (layer, 0, j))],
        out_specs=pl.BlockSpec((tm, tn), lambda j, i: (i, j)),
        out_shape=jax.ShapeDtypeStruct((m, n), out_dtype),
        scratch_shapes=[pltpu.VMEM((k, tn), BF16)],
        compiler_params=_cparams(2),
    )(a, w)


IN_SHIFT_QKVG = SSD_N_HEADS
IN_SHIFT_GATES = SSD_N_HEADS + GLA_GATE_RANK


def _in_proj_kernel(a_ref, w_ref, o_ref, wt_ref):
    @pl.when(pl.program_id(1) == 0)
    def _():
        _cast_weight(wt_ref, w_ref.at[0])

    o_ref[...] = lax.dot_general(a_ref[...], wt_ref[...], (((1,), (1,)), ((), ())),
                                 preferred_element_type=F32).astype(o_ref.dtype)


def _in_proj(a, w_in_t, layer, *, tm=2048, tn=1024):
    m, k = a.shape
    assert OFF_Q % tn == 0 and OFF_GS % tn == 0 and N_MAIN % tn == 0

    def w_rows(j, i):
        past_dt = (j >= OFF_Q // tn).astype(jnp.int32)
        past_gk = (j >= OFF_GS // tn).astype(jnp.int32)
        shift = past_dt * IN_SHIFT_QKVG + past_gk * (IN_SHIFT_GATES - IN_SHIFT_QKVG)
        return layer, pl.multiple_of(j * tn + shift, 16), 0

    return pl.pallas_call(
        _in_proj_kernel,
        grid=(N_MAIN // tn, m // tm),
        in_specs=[pl.BlockSpec((tm, k), lambda j, i: (i, 0)),
                  pl.BlockSpec((pl.Element(1), pl.Element(tn), pl.Element(k)), w_rows)],
        out_specs=pl.BlockSpec((tm, tn), lambda j, i: (i, j)),
        out_shape=jax.ShapeDtypeStruct((m, N_MAIN), BF16),
        scratch_shapes=[pltpu.VMEM((tn, k), BF16)],
        compiler_params=_cparams(2),
    )(a, w_in_t)


def _small_proj_kernel(a_ref, b_ref, o_ref, ot_ref):
    acc = lax.dot_general(a_ref[...], b_ref[...].astype(BF16), (((1,), (1,)), ((), ())),
                          preferred_element_type=F32)
    o_ref[...] = acc
    ot_ref[...] = acc.T


def _small_proj(a, b_t, layer, *, tm=1024):
    m, k = a.shape
    return pl.pallas_call(
        _small_proj_kernel,
        grid=(m // tm,),
        in_specs=[pl.BlockSpec((tm, k), lambda i: (i, 0)),
                  pl.BlockSpec((None, LANE, k), lambda i: (layer, 0, 0))],
        out_specs=[pl.BlockSpec((tm, LANE), lambda i: (i, 0)),
                   pl.BlockSpec((LANE, tm), lambda i: (0, i))],
        out_shape=[jax.ShapeDtypeStruct((m, LANE), F32),
                   jax.ShapeDtypeStruct((LANE, m), F32)],
        compiler_params=_cparams(1),
    )(a, b_t)


def _merge_kernel(y_ref, o_ref, ws_ref, wg_ref, gs_ref, gg_ref, gb_ref, out_ref, wst_ref, wgt_ref):
    @pl.when(pl.program_id(1) == 0)
    def _():
        _cast_weight(wst_ref, ws_ref)
        _cast_weight(wgt_ref, wg_ref)

    bs = jnp.dot(y_ref[...], wst_ref[...], preferred_element_type=F32)
    bg = jnp.dot(o_ref[...], wgt_ref[...], preferred_element_type=F32)
    gb = gb_ref[...]
    s_gate = _sigmoid(gs_ref[...].astype(F32) + gb[0:1, :])
    g_gate = _sigmoid(gg_ref[...].astype(F32) + gb[1:2, :])
    out_ref[...] = (s_gate * bs + g_gate * bg).astype(out_ref.dtype)


def _merge(y, o, proj, w_ssd, w_gla, gate_bias, layer, *, tm=512, tn=512):
    t = y.shape[0]
    return pl.pallas_call(
        _merge_kernel,
        grid=(D_MODEL // tn, t // tm),
        in_specs=[pl.BlockSpec((tm, SSD_D_INNER), lambda j, i: (i, 0)),
                  pl.BlockSpec((tm, GLA_D_VALUE), lambda j, i: (i, 0)),
                  pl.BlockSpec((None, SSD_D_INNER, tn), lambda j, i: (layer, 0, j)),
                  pl.BlockSpec((None, GLA_D_VALUE, tn), lambda j, i: (layer, 0, j)),
                  pl.BlockSpec((tm, tn), lambda j, i: (i, OFF_GS // tn + j)),
                  pl.BlockSpec((tm, tn), lambda j, i: (i, OFF_GG // tn + j)),
                  pl.BlockSpec((None, 2, tn), lambda j, i: (layer, 0, j))],
        out_specs=pl.BlockSpec((tm, tn), lambda j, i: (i, j)),
        out_shape=jax.ShapeDtypeStruct((t, D_MODEL), BF16),
        scratch_shapes=[pltpu.VMEM((SSD_D_INNER, tn), BF16), pltpu.VMEM((GLA_D_VALUE, tn), BF16)],
        compiler_params=_cparams(2),
    )(y, o, w_ssd, w_gla, proj, proj, gate_bias)


def _proj_ln_kernel(a_ref, w_ref, res_ref, g_ref, b_ref, of_ref, ob_ref):
    acc = jnp.dot(a_ref[...], w_ref[...], preferred_element_type=F32)
    out = _layer_norm(ALPHA * res_ref[...] + acc, g_ref[...], b_ref[...])
    of_ref[...] = out
    ob_ref[...] = out.astype(BF16)


def _proj_ln(a, w, res, g, b, layer, *, tm=512):
    t, k = a.shape
    return pl.pallas_call(
        _proj_ln_kernel,
        grid=(t // tm,),
        in_specs=[pl.BlockSpec((tm, k), lambda i: (i, 0)),
                  pl.BlockSpec((None, k, D_MODEL), lambda i: (layer, 0, 0)),
                  pl.BlockSpec((tm, D_MODEL), lambda i: (i, 0)),
                  pl.BlockSpec((None, 1, D_MODEL), lambda i: (layer, 0, 0)),
                  pl.BlockSpec((None, 1, D_MODEL), lambda i: (layer, 0, 0))],
        out_specs=[pl.BlockSpec((tm, D_MODEL), lambda i: (i, 0)),
                   pl.BlockSpec((tm, D_MODEL), lambda i: (i, 0))],
        out_shape=[jax.ShapeDtypeStruct((t, D_MODEL), F32),
                   jax.ShapeDtypeStruct((t, D_MODEL), BF16)],
        compiler_params=_cparams(1),
    )(a, w, res, g, b)


def _down_ln_kernel(u_ref, w_ref, res_ref, g_ref, b_ref, of_ref, ob_ref, acc_ref):
    kk = pl.program_id(1)

    @pl.when(kk == 0)
    def _():
        acc_ref[...] = jnp.zeros_like(acc_ref)

    acc_ref[...] += jnp.dot(u_ref[...], w_ref[...], preferred_element_type=F32)

    @pl.when(kk == pl.num_programs(1) - 1)
    def _():
        out = _layer_norm(ALPHA * res_ref[...] + acc_ref[...], g_ref[...], b_ref[...])
        of_ref[...] = out
        ob_ref[...] = out.astype(BF16)


def _down_ln(u, w, res, g, b, layer, *, tm=512, tk=2048):
    t, k = u.shape
    return pl.pallas_call(
        _down_ln_kernel,
        grid=(t // tm, k // tk),
        in_specs=[pl.BlockSpec((tm, tk), lambda i, kk: (i, kk)),
                  pl.BlockSpec((None, tk, D_MODEL), lambda i, kk: (layer, kk, 0)),
                  pl.BlockSpec((tm, D_MODEL), lambda i, kk: (i, 0)),
                  pl.BlockSpec((None, 1, D_MODEL), lambda i, kk: (layer, 0, 0)),
                  pl.BlockSpec((None, 1, D_MODEL), lambda i, kk: (layer, 0, 0))],
        out_specs=[pl.BlockSpec((tm, D_MODEL), lambda i, kk: (i, 0)),
                   pl.BlockSpec((tm, D_MODEL), lambda i, kk: (i, 0))],
        out_shape=[jax.ShapeDtypeStruct((t, D_MODEL), F32),
                   jax.ShapeDtypeStruct((t, D_MODEL), BF16)],
        scratch_shapes=[pltpu.VMEM((tm, D_MODEL), F32)],
        compiler_params=_cparams(2),
    )(u, w, res, g, b)


_EXP_SKIP = 1
_EXP_SECTIONS = 4


def _ssd_expand_matrix():
    r = lax.broadcasted_iota(jnp.int32, (LANE, _EXP_SECTIONS * SSD_GROUP_W), 0)
    c = lax.broadcasted_iota(jnp.int32, (LANE, _EXP_SECTIONS * SSD_GROUP_W), 1)
    src = (c // SSD_GROUP_W + _EXP_SKIP) * SSD_HEADS_PER_GROUP + (c % SSD_GROUP_W) // SSD_HEAD_DIM
    e = (r == src).astype(BF16)
    return jnp.concatenate([e, e], axis=0)


def _ssd_shift_matrix():
    q = SSD_CHUNK
    r = lax.broadcasted_iota(jnp.int32, (3 * q, 2 * q), 0)
    c = lax.broadcasted_iota(jnp.int32, (3 * q, 2 * q), 1)
    return (c == q + r % q - (r // q + 1)).astype(BF16)


def _split3(v):
    hi = v.astype(BF16)
    r1 = v - hi.astype(F32)
    mid = r1.astype(BF16)
    lo = (r1 - mid.astype(F32)).astype(BF16)
    return hi, mid, lo


def _ssd_kernel(x_ref, b_ref, c_ref, z_ref, dt_ref, cw_ref, cb_ref, dtb_ref, alog_ref,
                dexp_ref, nw_ref, e_ref, shift_ref, y_ref, tail_ref, st_ref, *, chunks_per_step):
    q = SSD_CHUNK
    hg = SSD_HEADS_PER_GROUP
    p = SSD_HEAD_DIM
    gw = SSD_GROUP_W
    n = SSD_D_STATE

    @pl.when(pl.program_id(2) == 0)
    def _():
        tail_ref[...] = jnp.zeros_like(tail_ref)
        st_ref[...] = jnp.zeros_like(st_ref)

    row = lax.broadcasted_iota(jnp.int32, (q, q), 0)
    col = lax.broadcasted_iota(jnp.int32, (q, q), 1)
    causal = col <= row
    tri_b = causal.astype(BF16)
    even_lanes = col < p
    nt = (((1,), (1,)), ((), ()))

    cw = cw_ref[...]
    cbias = cb_ref[...]
    dtb = dtb_ref[...]
    a_neg = -jnp.exp(alog_ref[...])
    dexp = dexp_ref[...]
    nw = nw_ref[...]
    shift = shift_ref[...]

    def dot(a, b, dims=None):
        if dims is None:
            return jnp.dot(a, b, preferred_element_type=F32)
        return lax.dot_general(a, b, dims, preferred_element_type=F32)

    cs_ = range(chunks_per_step)
    rows = [pl.ds(cc * q, q) for cc in cs_]

    raw = [jnp.concatenate([x_ref[r, :], b_ref[r, :], c_ref[r, :]], axis=1) for r in rows]
    prev = [tail_ref[...]] + raw[:-1]
    sh = [dot(shift, jnp.concatenate([prev[cc], raw[cc]], axis=0)) for cc in cs_]
    xbc = []
    for cc in cs_:
        acc = cbias + cw[3:4, :] * raw[cc].astype(F32)
        for s in range(1, SSD_CONV_W):
            acc = acc + cw[3 - s:4 - s, :] * sh[cc][(s - 1) * q:s * q, :]
        xbc.append(_silu(acc))
    xs = [v[:, 0:gw] for v in xbc]
    bm_b = [v[:, gw:gw + n].astype(BF16) for v in xbc]
    cm_b = [v[:, gw + n:].astype(BF16) for v in xbc]

    dt_t = [_softplus(dt_ref[:, cc * q:(cc + 1) * q] + dtb) for cc in cs_]
    a_cs_t = [sum(dot(part, tri_b, nt) for part in _split3(v * a_neg))
              for v in dt_t]
    cs = []
    for cc in cs_:
        a_last = a_cs_t[cc][:, q - 1:q]
        stack = jnp.concatenate(
            [a_cs_t[cc], dt_t[cc], jnp.exp(a_cs_t[cc]), jnp.exp(a_last - a_cs_t[cc]),
             jnp.broadcast_to(jnp.exp(a_last), (hg, q)),
             jnp.zeros((q - (_EXP_SECTIONS + _EXP_SKIP) * hg, q), F32)], axis=0)
        cs.append(stack.T)
    hilo = []
    for v in cs:
        hi = v.astype(BF16)
        hilo.append(jnp.concatenate([hi, (v - hi.astype(F32)).astype(BF16)], axis=1))
    ex = [dot(v, e_ref[:, 0:3 * gw]) for v in hilo]
    el_exp = [dot(v[0:hg, :], e_ref[:, 3 * gw:4 * gw])[0:1, :] for v in hilo]

    xdt = [xs[cc] * ex[cc][:, 0:gw] for cc in cs_]
    xdt_b = [v.astype(BF16) for v in xdt]
    xw_b = [(xdt[cc] * ex[cc][:, 2 * gw:3 * gw]).astype(BF16) for cc in cs_]
    cbm = [jnp.where(causal, dot(cm_b[cc], bm_b[cc], nt), 0.0) for cc in cs_]
    states = [dot(bm_b[cc], xw_b[cc], (((0,), (0,)), ((), ()))) for cc in cs_]

    y_diag = []
    for cc in cs_:
        ys = []
        for hp in range(hg // 2):
            l_mats = []
            for hh in (2 * hp, 2 * hp + 1):
                seg = cs[cc][:, hh:hh + 1] - a_cs_t[cc][hh:hh + 1, :]
                l_mats.append((cbm[cc] * jnp.exp(jnp.minimum(seg, 0.0))).astype(BF16))
            x_pair = xdt_b[cc][:, 2 * hp * p:(2 * hp + 2) * p]
            zero = jnp.zeros_like(x_pair)
            rhs = jnp.concatenate([jnp.where(even_lanes, x_pair, zero),
                                   jnp.where(even_lanes, zero, x_pair)], axis=0)
            ys.append(dot(jnp.concatenate(l_mats, axis=1), rhs))
        y_diag.append(jnp.concatenate(ys, axis=1))

    st = st_ref[...]
    y_off = []
    for cc in cs_:
        y_off.append(dot(cm_b[cc], st.astype(BF16)) * ex[cc][:, gw:2 * gw])
        st = st * el_exp[cc] + states[cc]
    st_ref[...] = st
    tail_ref[...] = raw[-1]

    for cc in cs_:
        y = y_diag[cc] + y_off[cc] + dexp * xs[cc]
        y = y * _silu(z_ref[rows[cc], :].astype(F32))
        y = y * lax.rsqrt(jnp.mean(y * y, axis=-1, keepdims=True) + RMS_EPS) * nw
        y_ref[rows[cc], :] = y.astype(y_ref.dtype)


def _ssd(proj, small_t, cw_g, cb_g, dtb_g, alog_g, dexp_g, nw_g, *, batch, seq, chunks_per_step=8):
    rows = SSD_CHUNK * chunks_per_step
    steps = seq // rows
    gw = SSD_GROUP_W
    n = SSD_D_STATE
    hg = SSD_HEADS_PER_GROUP
    e_mat = _ssd_expand_matrix()
    shift = _ssd_shift_matrix()

    def row_blk(b, c):
        return b * steps + c

    return pl.pallas_call(
        functools.partial(_ssd_kernel, chunks_per_step=chunks_per_step),
        grid=(batch, SSD_N_GROUPS, steps),
        in_specs=[
            pl.BlockSpec((rows, gw), lambda b, g, c: (row_blk(b, c), OFF_X // gw + g)),
            pl.BlockSpec((rows, n), lambda b, g, c: (row_blk(b, c), OFF_B // n + g)),
            pl.BlockSpec((rows, n), lambda b, g, c: (row_blk(b, c), OFF_C // n + g)),
            pl.BlockSpec((rows, gw), lambda b, g, c: (row_blk(b, c), OFF_Z // gw + g)),
            pl.BlockSpec((hg, rows), lambda b, g, c: (g, row_blk(b, c))),
            pl.BlockSpec((None, SSD_CONV_W, SSD_GROUP_CONV_W), lambda b, g, c: (g, 0, 0)),
            pl.BlockSpec((None, 1, SSD_GROUP_CONV_W), lambda b, g, c: (g, 0, 0)),
            pl.BlockSpec((None, hg, LANE), lambda b, g, c: (g, 0, 0)),
            pl.BlockSpec((None, hg, LANE), lambda b, g, c: (g, 0, 0)),
            pl.BlockSpec((None, 1, gw), lambda b, g, c: (g, 0, 0)),
            pl.BlockSpec((None, 1, gw), lambda b, g, c: (g, 0, 0)),
            pl.BlockSpec(e_mat.shape, lambda b, g, c: (0, 0)),
            pl.BlockSpec(shift.shape, lambda b, g, c: (0, 0)),
        ],
        out_specs=pl.BlockSpec((rows, gw), lambda b, g, c: (row_blk(b, c), g)),
        out_shape=jax.ShapeDtypeStruct((batch * seq, SSD_D_INNER), BF16),
        scratch_shapes=[pltpu.VMEM((SSD_CHUNK, SSD_GROUP_CONV_W), BF16),
                        pltpu.VMEM((SSD_D_STATE, gw), F32)],
        compiler_params=_cparams(3),
    )(proj, proj, proj, proj, small_t, cw_g, cb_g, dtb_g, alog_g, dexp_g, nw_g, e_mat, shift)


def _gla_cumsum_matrix(rows):
    r = lax.broadcasted_iota(jnp.int32, (rows, rows), 0)
    c = lax.broadcasted_iota(jnp.int32, (rows, rows), 1)
    return jnp.logical_and(r // GLA_CHUNK == c // GLA_CHUNK, c <= r).astype(F32)


def _gla_kernel(q_ref, k_ref, v_ref, g_ref, sm_ref, gkw_ref, gkb_ref, nw_ref, tri_ref, trib_ref,
                o_ref, st_ref, *, chunks_per_step):
    @pl.when(pl.program_id(1) == 0)
    def _():
        st_ref[...] = jnp.zeros_like(st_ref)

    c = GLA_CHUNK
    hk, hv = GLA_HEAD_K, GLA_HEAD_V
    hps = GLA_HEADS_PER_STEP
    scale = hk ** -0.5
    nt = (((1,), (1,)), ((), ()))
    tn = (((0,), (0,)), ((), ()))
    bs = range(q_ref.shape[0])
    chains = [(b, h) for b in bs for h in range(hps)]
    ch = range(len(chains))

    def dot(a, b, dims=None):
        if dims is None:
            return jnp.dot(a, b, preferred_element_type=F32)
        return lax.dot_general(a, b, dims, preferred_element_type=F32)

    def kcols(h):
        return slice(h * hk, (h + 1) * hk)

    def vcols(h):
        return slice(h * hv, (h + 1) * hv)

    tri = tri_ref[...]
    tri_b = trib_ref[...]
    nw = nw_ref[...]
    w = [_split3(gkw_ref[:, kcols(h)]) for h in range(hps)]
    sm = [_split3(sm_ref[b]) for b in bs]
    pre = [dot(sm[b][0], w[h][0]) + dot(sm[b][0], w[h][1]) + dot(sm[b][1], w[h][0])
           + gkb_ref[:, kcols(h)] for b, h in chains]
    gk = [-_softplus(-p) * (1.0 / GLA_GATE_NORMALIZER) for p in pre]
    gcs = [sum(dot(tri_b, part) for part in _split3(g)) for g in gk]
    kf = [k_ref[b, :, kcols(h)].astype(F32) for b, h in chains]
    vb = [v_ref[b, :, vcols(h)] for b, h in chains]
    qg = [(q_ref[b, :, kcols(h)].astype(F32) * scale * jnp.exp(gcs[i])).astype(BF16)
          for i, (b, h) in enumerate(chains)]
    kg = [(kf[i] * jnp.exp(-gcs[i])).astype(BF16) for i in ch]
    att = [(dot(qg[i], kg[i], nt) * tri).astype(BF16) for i in ch]
    o_intra = [dot(att[i], vb[i]) for i in ch]

    st = [st_ref[i] for i in ch]
    outs = [[] for _ in ch]
    for cc in range(chunks_per_step):
        rs = slice(cc * c, (cc + 1) * c)
        for i in ch:
            g_last = gcs[i][(cc + 1) * c - 1:(cc + 1) * c, :]
            kd = (kf[i][rs] * jnp.exp(g_last - gcs[i][rs])).astype(BF16)
            outs[i].append(o_intra[i][rs] + dot(qg[i][rs], st[i].astype(BF16), nt))
            st[i] = st[i] * jnp.exp(g_last) + dot(vb[i][rs], kd, tn)
    for i in ch:
        st_ref[i] = st[i]
    for i, (b, h) in enumerate(chains):
        o = jnp.concatenate(outs[i], axis=0)
        o = o * lax.rsqrt(jnp.mean(o * o, axis=-1, keepdims=True) + RMS_EPS) * nw
        o_ref[b, :, vcols(h)] = (o * _silu(g_ref[b, :, vcols(h)].astype(F32))).astype(o_ref.dtype)


def _gla(proj, small, gkw_pad, gkb, nw, *, batch, seq, chunks_per_step=4):
    rows = GLA_CHUNK * chunks_per_step
    steps = seq // rows
    hps = GLA_HEADS_PER_STEP
    wk, wv = hps * GLA_HEAD_K, hps * GLA_HEAD_V
    tri = _gla_cumsum_matrix(rows)

    proj3 = proj.reshape(batch, seq, N_MAIN)
    small3 = small.reshape(batch, seq, LANE)
    out = pl.pallas_call(
        functools.partial(_gla_kernel, chunks_per_step=chunks_per_step),
        grid=(GLA_N_HEADS // hps, steps),
        in_specs=[
            pl.BlockSpec((batch, rows, wk), lambda h, c: (0, c, OFF_Q // wk + h)),
            pl.BlockSpec((batch, rows, wk), lambda h, c: (0, c, OFF_K // wk + h)),
            pl.BlockSpec((batch, rows, wv), lambda h, c: (0, c, OFF_V // wv + h)),
            pl.BlockSpec((batch, rows, wv), lambda h, c: (0, c, OFF_G // wv + h)),
            pl.BlockSpec((batch, rows, LANE), lambda h, c: (0, c, 0)),
            pl.BlockSpec((LANE, wk), lambda h, c: (0, h)),
            pl.BlockSpec((1, wk), lambda h, c: (0, h)),
            pl.BlockSpec((1, GLA_HEAD_V), lambda h, c: (0, 0)),
            pl.BlockSpec((rows, rows), lambda h, c: (0, 0)),
            pl.BlockSpec((rows, rows), lambda h, c: (0, 0)),
        ],
        out_specs=pl.BlockSpec((batch, rows, wv), lambda h, c: (0, c, h)),
        out_shape=jax.ShapeDtypeStruct((batch, seq, GLA_D_VALUE), BF16),
        scratch_shapes=[pltpu.VMEM((batch * hps, GLA_HEAD_V, GLA_HEAD_K), F32)],
        compiler_params=_cparams(2),
    )(proj3, proj3, proj3, proj3, small3, gkw_pad, gkb, nw, tri, tri.astype(BF16))
    return out.reshape(batch * seq, GLA_D_VALUE)


def _small_in_proj_weight(w_in_t):
    c0 = OFF_Q
    c1 = c0 + SSD_N_HEADS
    c2 = c1 + 2 * GLA_D_KEY + 2 * GLA_D_VALUE
    c3 = c2 + GLA_GATE_RANK
    layers, _, k = w_in_t.shape
    pad = jnp.zeros((layers, LANE - SSD_N_HEADS - GLA_GATE_RANK, k), w_in_t.dtype)
    return jnp.concatenate([w_in_t[:, c0:c1, :], w_in_t[:, c2:c3, :], pad], axis=1)


def _head_rows(v):
    vg = v.reshape(SSD_N_GROUPS, SSD_HEADS_PER_GROUP, 1)
    return jnp.broadcast_to(vg, (SSD_N_GROUPS, SSD_HEADS_PER_GROUP, LANE))


def _layer(layer, xf, xb, batch, seq, w_in, w_small, conv_w, conv_b, dt_bias, a_log, d_skip, ssd_norm_w,
           gk_w, gk_b, gla_norm_w, w_ssd, w_gla, gate_bias, w_out_b, ln1_g, ln1_b,
           w_up, w_down_b, ln2_g, ln2_b):
    proj = _in_proj(xb, w_in, layer)
    small, small_t = _small_proj(xb, w_small, layer)

    gw, n = SSD_GROUP_W, SSD_D_STATE
    ng = SSD_N_GROUPS

    def conv_cols(v):
        lead = v.shape[:-1]
        xs = v[..., :SSD_D_INNER].reshape(lead + (ng, gw))
        bs = v[..., SSD_D_INNER:SSD_D_INNER + ng * n].reshape(lead + (ng, n))
        cs = v[..., SSD_D_INNER + ng * n:].reshape(lead + (ng, n))
        return jnp.moveaxis(jnp.concatenate([xs, bs, cs], axis=-1), -2, 0)

    cw_g = conv_cols(conv_w)
    cb_g = conv_cols(conv_b[None, :])
    dtb_g = _head_rows(dt_bias)
    alog_g = _head_rows(a_log)
    dexp_g = jnp.repeat(d_skip, SSD_HEAD_DIM).reshape(ng, 1, gw)
    nw_g = ssd_norm_w.reshape(ng, 1, gw)
    y = _ssd(proj, small_t, cw_g, cb_g, dtb_g, alog_g, dexp_g, nw_g, batch=batch, seq=seq)

    gkw_pad = jnp.zeros((LANE, GLA_D_KEY), F32).at[SMALL_GK:SMALL_GK + GLA_GATE_RANK].set(gk_w)
    o = _gla(proj, small, gkw_pad, gk_b[None, :], gla_norm_w[None, :], batch=batch, seq=seq)

    merged = _merge(y, o, proj, w_ssd, w_gla, gate_bias, layer)
    hf, hb = _proj_ln(merged, w_out_b, xf, ln1_g, ln1_b, layer)
    u = _matmul(hb, w_up, layer, tm=2048, tn=1024, out_dtype=BF16, act="relu2")
    return _down_ln(u, w_down_b, hf, ln2_g, ln2_b, layer)


def kernel(x, w_in, ssd_conv_w, ssd_conv_b, ssd_dt_bias, ssd_A_log, ssd_D, ssd_norm_w, gla_gk_w, gla_gk_b, gla_norm_w, w_ssd_branch, w_gla_branch, gate_bias, w_out, ln1_g, ln1_b, w_up, w_down, ln2_g, ln2_b):
    batch, seq, d = x.shape
    xf = x.reshape(batch * seq, d)
    xb = xf.astype(BF16)
    w_out_b = w_out.astype(BF16)
    w_down_b = w_down.astype(BF16)
    ln1_g, ln1_b, ln2_g, ln2_b = (v[:, None, :] for v in (ln1_g, ln1_b, ln2_g, ln2_b))
    w_in_t = jnp.swapaxes(w_in, 1, 2)
    w_small = _small_in_proj_weight(w_in_t)
    for l in range(w_in.shape[0]):
        xf, xb = _layer(l, xf, xb, batch, seq, w_in_t, w_small, ssd_conv_w[l], ssd_conv_b[l], ssd_dt_bias[l],
                        ssd_A_log[l], ssd_D[l], ssd_norm_w[l], gla_gk_w[l], gla_gk_b[l],
                        gla_norm_w[l], w_ssd_branch, w_gla_branch, gate_bias, w_out_b,
                        ln1_g, ln1_b, w_up, w_down_b, ln2_g, ln2_b)
    return xf.reshape(batch, seq, d)
```

```python
import functools

import jax
import jax.numpy as jnp
from jax import lax
from jax.experimental import pallas as pl
from jax.experimental.pallas import tpu as pltpu

F32 = jnp.float32
BF16 = jnp.bfloat16
HIGHEST = lax.Precision.HIGHEST

D_MODEL = 2048
DEPTH = 4
ALPHA = (2.0 * DEPTH) ** 0.25
LN_EPS = 1e-5
RMS_EPS = 1e-5

SSD_D_INNER = 2 * D_MODEL
SSD_HEAD_DIM = 64
SSD_N_HEADS = SSD_D_INNER // SSD_HEAD_DIM
SSD_N_GROUPS = 8
SSD_HEADS_PER_GROUP = SSD_N_HEADS // SSD_N_GROUPS
SSD_D_STATE = 128
SSD_CONV_W = 4
SSD_CHUNK = 128
SSD_GROUP_W = SSD_HEADS_PER_GROUP * SSD_HEAD_DIM
SSD_GROUP_CONV_W = SSD_GROUP_W + 2 * SSD_D_STATE

GLA_N_HEADS = 4
GLA_D_KEY = D_MODEL // 2
GLA_D_VALUE = D_MODEL
GLA_HEAD_K = GLA_D_KEY // GLA_N_HEADS
GLA_HEAD_V = GLA_D_VALUE // GLA_N_HEADS
GLA_GATE_RANK = 16
GLA_GATE_NORMALIZER = 16.0
GLA_CHUNK = 64
GLA_HEADS_PER_STEP = 4

D_FF = 4 * D_MODEL

OFF_Z = 0
OFF_X = OFF_Z + SSD_D_INNER
OFF_B = OFF_X + SSD_D_INNER
OFF_C = OFF_B + SSD_N_GROUPS * SSD_D_STATE
OFF_Q = OFF_C + SSD_N_GROUPS * SSD_D_STATE
OFF_K = OFF_Q + GLA_D_KEY
OFF_V = OFF_K + GLA_D_KEY
OFF_G = OFF_V + GLA_D_VALUE
OFF_GS = OFF_G + GLA_D_VALUE
OFF_GG = OFF_GS + D_MODEL
N_MAIN = OFF_GG + D_MODEL
LANE = 128
SMALL_DT = 0
SMALL_GK = SSD_N_HEADS

VMEM_LIMIT = 56 * 1024 * 1024


def _cparams(n_axes):
    return pltpu.CompilerParams(dimension_semantics=("arbitrary",) * n_axes,
                                vmem_limit_bytes=VMEM_LIMIT)


def _sigmoid(x):
    return 0.5 + 0.5 * jnp.tanh(0.5 * x)


def _silu(x):
    h = 0.5 * x
    return h + h * jnp.tanh(h)


def _softplus(x):
    return jnp.maximum(x, 0.0) + jnp.log(1.0 + jnp.exp(-jnp.abs(x)))


def _layer_norm(v, g, b):
    mu = jnp.mean(v, axis=-1, keepdims=True)
    vc = v - mu
    var = jnp.mean(vc * vc, axis=-1, keepdims=True)
    return vc * lax.rsqrt(var + LN_EPS) * g + b


CAST_ROWS = 256


def _cast_weight(dst_ref, src_ref):
    def body(r, carry):
        rows = pl.ds(pl.multiple_of(r * CAST_ROWS, CAST_ROWS), CAST_ROWS)
        dst_ref[rows, :] = src_ref[rows, :].astype(BF16)
        return carry
    lax.fori_loop(0, src_ref.shape[0] // CAST_ROWS, body, 0)


def _mm_kernel(a_ref, w_ref, o_ref, wt_ref, *, act):
    @pl.when(pl.program_id(1) == 0)
    def _():
        _cast_weight(wt_ref, w_ref)

    acc = jnp.dot(a_ref[...], wt_ref[...], preferred_element_type=F32)
    if act == "relu2":
        r = jnp.maximum(acc, 0.0)
        acc = r * r
    o_ref[...] = acc.astype(o_ref.dtype)


def _matmul(a, w, layer, *, tm, tn, out_dtype, act=None):
    m, k = a.shape
    n = w.shape[2]
    return pl.pallas_call(
        functools.partial(_mm_kernel, act=act),
        grid=(n // tn, m // tm),
        in_specs=[pl.BlockSpec((tm, k), lambda j, i: (i, 0)),
                  pl.BlockSpec((None, k, tn), lambda j, i: (layer, 0, j))],
        out_specs=pl.BlockSpec((tm, tn), lambda j, i: (i, j)),
        out_shape=jax.ShapeDtypeStruct((m, n), out_dtype),
        scratch_shapes=[pltpu.VMEM((k, tn), BF16)],
        compiler_params=_cparams(2),
    )(a, w)


IN_SHIFT_QKVG = SSD_N_HEADS
IN_SHIFT_GATES = SSD_N_HEADS + GLA_GATE_RANK


def _in_proj_kernel(a_ref, w_ref, o_ref, wt_ref):
    @pl.when(pl.program_id(1) == 0)
    def _():
        _cast_weight(wt_ref, w_ref.at[0])

    o_ref[...] = lax.dot_general(a_ref[...], wt_ref[...], (((1,), (1,)), ((), ())),
                                 preferred_element_type=F32).astype(o_ref.dtype)


def _in_proj(a, w_in_t, layer, *, tm=2048, tn=1024):
    m, k = a.shape
    assert OFF_Q % tn == 0 and OFF_GS % tn == 0 and N_MAIN % tn == 0

    def w_rows(j, i):
        past_dt = (j >= OFF_Q // tn).astype(jnp.int32)
        past_gk = (j >= OFF_GS // tn).astype(jnp.int32)
        shift = past_dt * IN_SHIFT_QKVG + past_gk * (IN_SHIFT_GATES - IN_SHIFT_QKVG)
        return layer, pl.multiple_of(j * tn + shift, 16), 0

    return pl.pallas_call(
        _in_proj_kernel,
        grid=(N_MAIN // tn, m // tm),
        in_specs=[pl.BlockSpec((tm, k), lambda j, i: (i, 0)),
                  pl.BlockSpec((pl.Element(1), pl.Element(tn), pl.Element(k)), w_rows)],
        out_specs=pl.BlockSpec((tm, tn), lambda j, i: (i, j)),
        out_shape=jax.ShapeDtypeStruct((m, N_MAIN), BF16),
        scratch_shapes=[pltpu.VMEM((tn, k), BF16)],
        compiler_params=_cparams(2),
    )(a, w_in_t)


def _small_proj_kernel(a_ref, b_ref, o_ref, ot_ref):
    acc = lax.dot_general(a_ref[...], b_ref[...].astype(BF16), (((1,), (1,)), ((), ())),
                          preferred_element_type=F32)
    o_ref[...] = acc
    ot_ref[...] = acc.T


def _small_proj(a, b_t, layer, *, tm=1024):
    m, k = a.shape
    return pl.pallas_call(
        _small_proj_kernel,
        grid=(m // tm,),
        in_specs=[pl.BlockSpec((tm, k), lambda i: (i, 0)),
                  pl.BlockSpec((None, LANE, k), lambda i: (layer, 0, 0))],
        out_specs=[pl.BlockSpec((tm, LANE), lambda i: (i, 0)),
                   pl.BlockSpec((LANE, tm), lambda i: (0, i))],
        out_shape=[jax.ShapeDtypeStruct((m, LANE), F32),
                   jax.ShapeDtypeStruct((LANE, m), F32)],
        compiler_params=_cparams(1),
    )(a, b_t)


def _merge_kernel(y_ref, o_ref, ws_ref, wg_ref, gs_ref, gg_ref, gb_ref, out_ref, wst_ref, wgt_ref):
    @pl.when(pl.program_id(1) == 0)
    def _():
        _cast_weight(wst_ref, ws_ref)
        _cast_weight(wgt_ref, wg_ref)

    bs = jnp.dot(y_ref[...], wst_ref[...], preferred_element_type=F32)
    bg = jnp.dot(o_ref[...], wgt_ref[...], preferred_element_type=F32)
    gb = gb_ref[...]
    s_gate = _sigmoid(gs_ref[...].astype(F32) + gb[0:1, :])
    g_gate = _sigmoid(gg_ref[...].astype(F32) + gb[1:2, :])
    out_ref[...] = (s_gate * bs + g_gate * bg).astype(out_ref.dtype)


def _merge(y, o, proj, w_ssd, w_gla, gate_bias, layer, *, tm=512, tn=512):
    t = y.shape[0]
    return pl.pallas_call(
        _merge_kernel,
        grid=(D_MODEL // tn, t // tm),
        in_specs=[pl.BlockSpec((tm, SSD_D_INNER), lambda j, i: (i, 0)),
                  pl.BlockSpec((tm, GLA_D_VALUE), lambda j, i: (i, 0)),
                  pl.BlockSpec((None, SSD_D_INNER, tn), lambda j, i: (layer, 0, j)),
                  pl.BlockSpec((None, GLA_D_VALUE, tn), lambda j, i: (layer, 0, j)),
                  pl.BlockSpec((tm, tn), lambda j, i: (i, OFF_GS // tn + j)),
                  pl.BlockSpec((tm, tn), lambda j, i: (i, OFF_GG // tn + j)),
                  pl.BlockSpec((None, 2, tn), lambda j, i: (layer, 0, j))],
        out_specs=pl.BlockSpec((tm, tn), lambda j, i: (i, j)),
        out_shape=jax.ShapeDtypeStruct((t, D_MODEL), BF16),
        scratch_shapes=[pltpu.VMEM((SSD_D_INNER, tn), BF16), pltpu.VMEM((GLA_D_VALUE, tn), BF16)],
        compiler_params=_cparams(2),
    )(y, o, w_ssd, w_gla, proj, proj, gate_bias)


def _proj_ln_kernel(a_ref, w_ref, res_ref, g_ref, b_ref, of_ref, ob_ref):
    acc = jnp.dot(a_ref[...], w_ref[...], preferred_element_type=F32)
    out = _layer_norm(ALPHA * res_ref[...] + acc, g_ref[...], b_ref[...])
    of_ref[...] = out
    ob_ref[...] = out.astype(BF16)


def _proj_ln(a, w, res, g, b, layer, *, tm=512):
    t, k = a.shape
    return pl.pallas_call(
        _proj_ln_kernel,
        grid=(t // tm,),
        in_specs=[pl.BlockSpec((tm, k), lambda i: (i, 0)),
                  pl.BlockSpec((None, k, D_MODEL), lambda i: (layer, 0, 0)),
                  pl.BlockSpec((tm, D_MODEL), lambda i: (i, 0)),
                  pl.BlockSpec((None, 1, D_MODEL), lambda i: (layer, 0, 0)),
                  pl.BlockSpec((None, 1, D_MODEL), lambda i: (layer, 0, 0))],
        out_specs=[pl.BlockSpec((tm, D_MODEL), lambda i: (i, 0)),
                   pl.BlockSpec((tm, D_MODEL), lambda i: (i, 0))],
        out_shape=[jax.ShapeDtypeStruct((t, D_MODEL), F32),
                   jax.ShapeDtypeStruct((t, D_MODEL), BF16)],
        compiler_params=_cparams(1),
    )(a, w, res, g, b)


def _down_ln_kernel(u_ref, w_ref, res_ref, g_ref, b_ref, of_ref, ob_ref, acc_ref):
    kk = pl.program_id(1)

    @pl.when(kk == 0)
    def _():
        acc_ref[...] = jnp.zeros_like(acc_ref)

    acc_ref[...] += jnp.dot(u_ref[...], w_ref[...], preferred_element_type=F32)

    @pl.when(kk == pl.num_programs(1) - 1)
    def _():
        out = _layer_norm(ALPHA * res_ref[...] + acc_ref[...], g_ref[...], b_ref[...])
        of_ref[...] = out
        ob_ref[...] = out.astype(BF16)


def _down_ln(u, w, res, g, b, layer, *, tm=512, tk=2048):
    t, k = u.shape
    return pl.pallas_call(
        _down_ln_kernel,
        grid=(t // tm, k // tk),
        in_specs=[pl.BlockSpec((tm, tk), lambda i, kk: (i, kk)),
                  pl.BlockSpec((None, tk, D_MODEL), lambda i, kk: (layer, kk, 0)),
                  pl.BlockSpec((tm, D_MODEL), lambda i, kk: (i, 0)),
                  pl.BlockSpec((None, 1, D_MODEL), lambda i, kk: (layer, 0, 0)),
                  pl.BlockSpec((None, 1, D_MODEL), lambda i, kk: (layer, 0, 0))],
        out_specs=[pl.BlockSpec((tm, D_MODEL), lambda i, kk: (i, 0)),
                   pl.BlockSpec((tm, D_MODEL), lambda i, kk: (i, 0))],
        out_shape=[jax.ShapeDtypeStruct((t, D_MODEL), F32),
                   jax.ShapeDtypeStruct((t, D_MODEL), BF16)],
        scratch_shapes=[pltpu.VMEM((tm, D_MODEL), F32)],
        compiler_params=_cparams(2),
    )(u, w, res, g, b)


_EXP_SKIP = 1
_EXP_SECTIONS = 4


def _ssd_expand_matrix():
    r = lax.broadcasted_iota(jnp.int32, (LANE, _EXP_SECTIONS * SSD_GROUP_W), 0)
    c = lax.broadcasted_iota(jnp.int32, (LANE, _EXP_SECTIONS * SSD_GROUP_W), 1)
    src = (c // SSD_GROUP_W + _EXP_SKIP) * SSD_HEADS_PER_GROUP + (c % SSD_GROUP_W) // SSD_HEAD_DIM
    e = (r == src).astype(BF16)
    return jnp.concatenate([e, e], axis=0)


def _ssd_shift_matrix():
    q = SSD_CHUNK
    r = lax.broadcasted_iota(jnp.int32, (3 * q, 2 * q), 0)
    c = lax.broadcasted_iota(jnp.int32, (3 * q, 2 * q), 1)
    return (c == q + r % q - (r // q + 1)).astype(BF16)


def _split3(v):
    hi = v.astype(BF16)
    r1 = v - hi.astype(F32)
    mid = r1.astype(BF16)
    lo = (r1 - mid.astype(F32)).astype(BF16)
    return hi, mid, lo


def _ssd_kernel(x_ref, b_ref, c_ref, z_ref, dt_ref, cw_ref, cb_ref, dtb_ref, alog_ref,
                dexp_ref, nw_ref, e_ref, shift_ref, y_ref, tail_ref, st_ref, *, chunks_per_step):
    q = SSD_CHUNK
    hg = SSD_HEADS_PER_GROUP
    p = SSD_HEAD_DIM
    gw = SSD_GROUP_W
    n = SSD_D_STATE

    @pl.when(pl.program_id(2) == 0)
    def _():
        tail_ref[...] = jnp.zeros_like(tail_ref)
        st_ref[...] = jnp.zeros_like(st_ref)

    row = lax.broadcasted_iota(jnp.int32, (q, q), 0)
    col = lax.broadcasted_iota(jnp.int32, (q, q), 1)
    causal = col <= row
    tri_b = causal.astype(BF16)
    even_lanes = col < p
    nt = (((1,), (1,)), ((), ()))

    cw = cw_ref[...]
    cbias = cb_ref[...]
    dtb = dtb_ref[...]
    a_neg = -jnp.exp(alog_ref[...])
    dexp = dexp_ref[...]
    nw = nw_ref[...]
    shift = shift_ref[...]

    def dot(a, b, dims=None):
        if dims is None:
            return jnp.dot(a, b, preferred_element_type=F32)
        return lax.dot_general(a, b, dims, preferred_element_type=F32)

    cs_ = range(chunks_per_step)
    rows = [pl.ds(cc * q, q) for cc in cs_]

    raw = [jnp.concatenate([x_ref[r, :], b_ref[r, :], c_ref[r, :]], axis=1) for r in rows]
    prev = [tail_ref[...]] + raw[:-1]
    sh = [dot(shift, jnp.concatenate([prev[cc], raw[cc]], axis=0)) for cc in cs_]
    xbc = []
    for cc in cs_:
        acc = cbias + cw[3:4, :] * raw[cc].astype(F32)
        for s in range(1, SSD_CONV_W):
            acc = acc + cw[3 - s:4 - s, :] * sh[cc][(s - 1) * q:s * q, :]
        xbc.append(_silu(acc))
    xs = [v[:, 0:gw] for v in xbc]
    bm_b = [v[:, gw:gw + n].astype(BF16) for v in xbc]
    cm_b = [v[:, gw + n:].astype(BF16) for v in xbc]

    dt_t = [_softplus(dt_ref[:, cc * q:(cc + 1) * q] + dtb) for cc in cs_]
    a_cs_t = [sum(dot(part, tri_b, nt) for part in _split3(v * a_neg))
              for v in dt_t]
    cs = []
    for cc in cs_:
        a_last = a_cs_t[cc][:, q - 1:q]
        stack = jnp.concatenate(
            [a_cs_t[cc], dt_t[cc], jnp.exp(a_cs_t[cc]), jnp.exp(a_last - a_cs_t[cc]),
             jnp.broadcast_to(jnp.exp(a_last), (hg, q)),
             jnp.zeros((q - (_EXP_SECTIONS + _EXP_SKIP) * hg, q), F32)], axis=0)
        cs.append(stack.T)
    hilo = []
    for v in cs:
        hi = v.astype(BF16)
        hilo.append(jnp.concatenate([hi, (v - hi.astype(F32)).astype(BF16)], axis=1))
    ex = [dot(v, e_ref[:, 0:3 * gw]) for v in hilo]
    el_exp = [dot(v[0:hg, :], e_ref[:, 3 * gw:4 * gw])[0:1, :] for v in hilo]

    xdt = [xs[cc] * ex[cc][:, 0:gw] for cc in cs_]
    xdt_b = [v.astype(BF16) for v in xdt]
    xw_b = [(xdt[cc] * ex[cc][:, 2 * gw:3 * gw]).astype(BF16) for cc in cs_]
    cbm = [jnp.where(causal, dot(cm_b[cc], bm_b[cc], nt), 0.0) for cc in cs_]
    states = [dot(bm_b[cc], xw_b[cc], (((0,), (0,)), ((), ()))) for cc in cs_]

    y_diag = []
    for cc in cs_:
        ys = []
        for hp in range(hg // 2):
            l_mats = []
            for hh in (2 * hp, 2 * hp + 1):
                seg = cs[cc][:, hh:hh + 1] - a_cs_t[cc][hh:hh + 1, :]
                l_mats.append((cbm[cc] * jnp.exp(jnp.minimum(seg, 0.0))).astype(BF16))
            x_pair = xdt_b[cc][:, 2 * hp * p:(2 * hp + 2) * p]
            zero = jnp.zeros_like(x_pair)
            rhs = jnp.concatenate([jnp.where(even_lanes, x_pair, zero),
                                   jnp.where(even_lanes, zero, x_pair)], axis=0)
            ys.append(dot(jnp.concatenate(l_mats, axis=1), rhs))
        y_diag.append(jnp.concatenate(ys, axis=1))

    st = st_ref[...]
    y_off = []
    for cc in cs_:
        y_off.append(dot(cm_b[cc], st.astype(BF16)) * ex[cc][:, gw:2 * gw])
        st = st * el_exp[cc] + states[cc]
    st_ref[...] = st
    tail_ref[...] = raw[-1]

    for cc in cs_:
        y = y_diag[cc] + y_off[cc] + dexp * xs[cc]
        y = y * _silu(z_ref[rows[cc], :].astype(F32))
        y = y * lax.rsqrt(jnp.mean(y * y, axis=-1, keepdims=True) + RMS_EPS) * nw
        y_ref[rows[cc], :] = y.astype(y_ref.dtype)


def _ssd(proj, small_t, cw_g, cb_g, dtb_g, alog_g, dexp_g, nw_g, *, batch, seq, chunks_per_step=16):
    rows = SSD_CHUNK * chunks_per_step
    steps = seq // rows
    gw = SSD_GROUP_W
    n = SSD_D_STATE
    hg = SSD_HEADS_PER_GROUP
    e_mat = _ssd_expand_matrix()
    shift = _ssd_shift_matrix()

    def row_blk(b, c):
        return b * steps + c

    return pl.pallas_call(
        functools.partial(_ssd_kernel, chunks_per_step=chunks_per_step),
        grid=(batch, SSD_N_GROUPS, steps),
        in_specs=[
            pl.BlockSpec((rows, gw), lambda b, g, c: (row_blk(b, c), OFF_X // gw + g)),
            pl.BlockSpec((rows, n), lambda b, g, c: (row_blk(b, c), OFF_B // n + g)),
            pl.BlockSpec((rows, n), lambda b, g, c: (row_blk(b, c), OFF_C // n + g)),
            pl.BlockSpec((rows, gw), lambda b, g, c: (row_blk(b, c), OFF_Z // gw + g)),
            pl.BlockSpec((hg, rows), lambda b, g, c: (g, row_blk(b, c))),
            pl.BlockSpec((None, SSD_CONV_W, SSD_GROUP_CONV_W), lambda b, g, c: (g, 0, 0)),
            pl.BlockSpec((None, 1, SSD_GROUP_CONV_W), lambda b, g, c: (g, 0, 0)),
            pl.BlockSpec((None, hg, LANE), lambda b, g, c: (g, 0, 0)),
            pl.BlockSpec((None, hg, LANE), lambda b, g, c: (g, 0, 0)),
            pl.BlockSpec((None, 1, gw), lambda b, g, c: (g, 0, 0)),
            pl.BlockSpec((None, 1, gw), lambda b, g, c: (g, 0, 0)),
            pl.BlockSpec(e_mat.shape, lambda b, g, c: (0, 0)),
            pl.BlockSpec(shift.shape, lambda b, g, c: (0, 0)),
        ],
        out_specs=pl.BlockSpec((rows, gw), lambda b, g, c: (row_blk(b, c), g)),
        out_shape=jax.ShapeDtypeStruct((batch * seq, SSD_D_INNER), BF16),
        scratch_shapes=[pltpu.VMEM((SSD_CHUNK, SSD_GROUP_CONV_W), BF16),
                        pltpu.VMEM((SSD_D_STATE, gw), F32)],
        compiler_params=_cparams(3),
    )(proj, proj, proj, proj, small_t, cw_g, cb_g, dtb_g, alog_g, dexp_g, nw_g, e_mat, shift)


def _gla_cumsum_matrix(rows):
    r = lax.broadcasted_iota(jnp.int32, (rows, rows), 0)
    c = lax.broadcasted_iota(jnp.int32, (rows, rows), 1)
    return jnp.logical_and(r // GLA_CHUNK == c // GLA_CHUNK, c <= r).astype(F32)


def _gla_kernel(q_ref, k_ref, v_ref, g_ref, sm_ref, gkw_ref, gkb_ref, nw_ref, tri_ref, trib_ref,
                o_ref, st_ref, *, chunks_per_step):
    @pl.when(pl.program_id(1) == 0)
    def _():
        st_ref[...] = jnp.zeros_like(st_ref)

    c = GLA_CHUNK
    hk, hv = GLA_HEAD_K, GLA_HEAD_V
    hps = GLA_HEADS_PER_STEP
    scale = hk ** -0.5
    nt = (((1,), (1,)), ((), ()))
    tn = (((0,), (0,)), ((), ()))
    bs = range(q_ref.shape[0])
    chains = [(b, h) for b in bs for h in range(hps)]
    ch = range(len(chains))

    def dot(a, b, dims=None):
        if dims is None:
            return jnp.dot(a, b, preferred_element_type=F32)
        return lax.dot_general(a, b, dims, preferred_element_type=F32)

    def kcols(h):
        return slice(h * hk, (h + 1) * hk)

    def vcols(h):
        return slice(h * hv, (h + 1) * hv)

    tri = tri_ref[...]
    tri_b = trib_ref[...]
    nw = nw_ref[...]
    w = [_split3(gkw_ref[:, kcols(h)]) for h in range(hps)]
    sm = [_split3(sm_ref[b]) for b in bs]
    pre = [dot(sm[b][0], w[h][0]) + dot(sm[b][0], w[h][1]) + dot(sm[b][1], w[h][0])
           + gkb_ref[:, kcols(h)] for b, h in chains]
    gk = [-_softplus(-p) * (1.0 / GLA_GATE_NORMALIZER) for p in pre]
    gcs = [sum(dot(tri_b, part) for part in _split3(g)) for g in gk]
    kf = [k_ref[b, :, kcols(h)].astype(F32) for b, h in chains]
    vb = [v_ref[b, :, vcols(h)] for b, h in chains]
    qg = [(q_ref[b, :, kcols(h)].astype(F32) * scale * jnp.exp(gcs[i])).astype(BF16)
          for i, (b, h) in enumerate(chains)]
    kg = [(kf[i] * jnp.exp(-gcs[i])).astype(BF16) for i in ch]
    att = [(dot(qg[i], kg[i], nt) * tri).astype(BF16) for i in ch]
    o_intra = [dot(att[i], vb[i]) for i in ch]

    st = [st_ref[i] for i in ch]
    outs = [[] for _ in ch]
    for cc in range(chunks_per_step):
        rs = slice(cc * c, (cc + 1) * c)
        for i in ch:
            g_last = gcs[i][(cc + 1) * c - 1:(cc + 1) * c, :]
            kd = (kf[i][rs] * jnp.exp(g_last - gcs[i][rs])).astype(BF16)
            outs[i].append(o_intra[i][rs] + dot(qg[i][rs], st[i].astype(BF16), nt))
            st[i] = st[i] * jnp.exp(g_last) + dot(vb[i][rs], kd, tn)
    for i in ch:
        st_ref[i] = st[i]
    for i, (b, h) in enumerate(chains):
        o = jnp.concatenate(outs[i], axis=0)
        o = o * lax.rsqrt(jnp.mean(o * o, axis=-1, keepdims=True) + RMS_EPS) * nw
        o_ref[b, :, vcols(h)] = (o * _silu(g_ref[b, :, vcols(h)].astype(F32))).astype(o_ref.dtype)


def _gla(proj, small, gkw_pad, gkb, nw, *, batch, seq, chunks_per_step=4):
    rows = GLA_CHUNK * chunks_per_step
    steps = seq // rows
    hps = GLA_HEADS_PER_STEP
    wk, wv = hps * GLA_HEAD_K, hps * GLA_HEAD_V
    tri = _gla_cumsum_matrix(rows)

    proj3 = proj.reshape(batch, seq, N_MAIN)
    small3 = small.reshape(batch, seq, LANE)
    out = pl.pallas_call(
        functools.partial(_gla_kernel, chunks_per_step=chunks_per_step),
        grid=(GLA_N_HEADS // hps, steps),
        in_specs=[
            pl.BlockSpec((batch, rows, wk), lambda h, c: (0, c, OFF_Q // wk + h)),
            pl.BlockSpec((batch, rows, wk), lambda h, c: (0, c, OFF_K // wk + h)),
            pl.BlockSpec((batch, rows, wv), lambda h, c: (0, c, OFF_V // wv + h)),
            pl.BlockSpec((batch, rows, wv), lambda h, c: (0, c, OFF_G // wv + h)),
            pl.BlockSpec((batch, rows, LANE), lambda h, c: (0, c, 0)),
            pl.BlockSpec((LANE, wk), lambda h, c: (0, h)),
            pl.BlockSpec((1, wk), lambda h, c: (0, h)),
            pl.BlockSpec((1, GLA_HEAD_V), lambda h, c: (0, 0)),
            pl.BlockSpec((rows, rows), lambda h, c: (0, 0)),
            pl.BlockSpec((rows, rows), lambda h, c: (0, 0)),
        ],
        out_specs=pl.BlockSpec((batch, rows, wv), lambda h, c: (0, c, h)),
        out_shape=jax.ShapeDtypeStruct((batch, seq, GLA_D_VALUE), BF16),
        scratch_shapes=[pltpu.VMEM((batch * hps, GLA_HEAD_V, GLA_HEAD_K), F32)],
        compiler_params=_cparams(2),
    )(proj3, proj3, proj3, proj3, small3, gkw_pad, gkb, nw, tri, tri.astype(BF16))
    return out.reshape(batch * seq, GLA_D_VALUE)


def _small_in_proj_weight(w_in_t):
    c0 = OFF_Q
    c1 = c0 + SSD_N_HEADS
    c2 = c1 + 2 * GLA_D_KEY + 2 * GLA_D_VALUE
    c3 = c2 + GLA_GATE_RANK
    layers, _, k = w_in_t.shape
    pad = jnp.zeros((layers, LANE - SSD_N_HEADS - GLA_GATE_RANK, k), w_in_t.dtype)
    return jnp.concatenate([w_in_t[:, c0:c1, :], w_in_t[:, c2:c3, :], pad], axis=1)


def _head_rows(v):
    vg = v.reshape(SSD_N_GROUPS, SSD_HEADS_PER_GROUP, 1)
    return jnp.broadcast_to(vg, (SSD_N_GROUPS, SSD_HEADS_PER_GROUP, LANE))


def _layer(layer, xf, xb, batch, seq, w_in, w_small, conv_w, conv_b, dt_bias, a_log, d_skip, ssd_norm_w,
           gk_w, gk_b, gla_norm_w, w_ssd, w_gla, gate_bias, w_out_b, ln1_g, ln1_b,
           w_up, w_down_b, ln2_g, ln2_b):
    proj = _in_proj(xb, w_in, layer)
    small, small_t = _small_proj(xb, w_small, layer)

    gw, n = SSD_GROUP_W, SSD_D_STATE
    ng = SSD_N_GROUPS

    def conv_cols(v):
        lead = v.shape[:-1]
        xs = v[..., :SSD_D_INNER].reshape(lead + (ng, gw))
        bs = v[..., SSD_D_INNER:SSD_D_INNER + ng * n].reshape(lead + (ng, n))
        cs = v[..., SSD_D_INNER + ng * n:].reshape(lead + (ng, n))
        return jnp.moveaxis(jnp.concatenate([xs, bs, cs], axis=-1), -2, 0)

    cw_g = conv_cols(conv_w)
    cb_g = conv_cols(conv_b[None, :])
    dtb_g = _head_rows(dt_bias)
    alog_g = _head_rows(a_log)
    dexp_g = jnp.repeat(d_skip, SSD_HEAD_DIM).reshape(ng, 1, gw)
    nw_g = ssd_norm_w.reshape(ng, 1, gw)
    y = _ssd(proj, small_t, cw_g, cb_g, dtb_g, alog_g, dexp_g, nw_g, batch=batch, seq=seq)

    gkw_pad = jnp.zeros((LANE, GLA_D_KEY), F32).at[SMALL_GK:SMALL_GK + GLA_GATE_RANK].set(gk_w)
    o = _gla(proj, small, gkw_pad, gk_b[None, :], gla_norm_w[None, :], batch=batch, seq=seq)

    merged = _merge(y, o, proj, w_ssd, w_gla, gate_bias, layer)
    hf, hb = _proj_ln(merged, w_out_b, xf, ln1_g, ln1_b, layer)
    u = _matmul(hb, w_up, layer, tm=2048, tn=1024, out_dtype=BF16, act="relu2")
    return _down_ln(u, w_down_b, hf, ln2_g, ln2_b, layer)


def kernel(x, w_in, ssd_conv_w, ssd_conv_b, ssd_dt_bias, ssd_A_log, ssd_D, ssd_norm_w, gla_gk_w, gla_gk_b, gla_norm_w, w_ssd_branch, w_gla_branch, gate_bias, w_out, ln1_g, ln1_b, w_up, w_down, ln2_g, ln2_b):
    batch, seq, d = x.shape
    xf = x.reshape(batch * seq, d)
    xb = xf.astype(BF16)
    w_out_b = w_out.astype(BF16)
    w_down_b = w_down.astype(BF16)
    ln1_g, ln1_b, ln2_g, ln2_b = (v[:, None, :] for v in (ln1_g, ln1_b, ln2_g, ln2_b))
    w_in_t = jnp.swapaxes(w_in, 1, 2)
    w_small = _small_in_proj_weight(w_in_t)
    for l in range(w_in.shape[0]):
        xf, xb = _layer(l, xf, xb, batch, seq, w_in_t, w_small, ssd_conv_w[l], ssd_conv_b[l], ssd_dt_bias[l],
                        ssd_A_log[l], ssd_D[l], ssd_norm_w[l], gla_gk_w[l], gla_gk_b[l],
                        gla_norm_w[l], w_ssd_branch, w_gla_branch, gate_bias, w_out_b,
                        ln1_g, ln1_b, w_up, w_down_b, ln2_g, ln2_b)
    return xf.reshape(batch, seq, d)
```

```python
import functools

import jax
import jax.numpy as jnp
from jax import lax
from jax.experimental import pallas as pl
from jax.experimental.pallas import tpu as pltpu

F32 = jnp.float32
BF16 = jnp.bfloat16
HIGHEST = lax.Precision.HIGHEST

D_MODEL = 2048
DEPTH = 4
ALPHA = (2.0 * DEPTH) ** 0.25
LN_EPS = 1e-5
RMS_EPS = 1e-5

SSD_D_INNER = 2 * D_MODEL
SSD_HEAD_DIM = 64
SSD_N_HEADS = SSD_D_INNER // SSD_HEAD_DIM
SSD_N_GROUPS = 8
SSD_HEADS_PER_GROUP = SSD_N_HEADS // SSD_N_GROUPS
SSD_D_STATE = 128
SSD_CONV_W = 4
SSD_CHUNK = 128
SSD_GROUP_W = SSD_HEADS_PER_GROUP * SSD_HEAD_DIM
SSD_GROUP_CONV_W = SSD_GROUP_W + 2 * SSD_D_STATE

GLA_N_HEADS = 4
GLA_D_KEY = D_MODEL // 2
GLA_D_VALUE = D_MODEL
GLA_HEAD_K = GLA_D_KEY // GLA_N_HEADS
GLA_HEAD_V = GLA_D_VALUE // GLA_N_HEADS
GLA_GATE_RANK = 16
GLA_GATE_NORMALIZER = 16.0
GLA_CHUNK = 64
GLA_HEADS_PER_STEP = 4

D_FF = 4 * D_MODEL

OFF_Z = 0
OFF_X = OFF_Z + SSD_D_INNER
OFF_B = OFF_X + SSD_D_INNER
OFF_C = OFF_B + SSD_N_GROUPS * SSD_D_STATE
OFF_Q = OFF_C + SSD_N_GROUPS * SSD_D_STATE
OFF_K = OFF_Q + GLA_D_KEY
OFF_V = OFF_K + GLA_D_KEY
OFF_G = OFF_V + GLA_D_VALUE
OFF_GS = OFF_G + GLA_D_VALUE
OFF_GG = OFF_GS + D_MODEL
N_MAIN = OFF_GG + D_MODEL
LANE = 128
SMALL_DT = 0
SMALL_GK = SSD_N_HEADS

VMEM_LIMIT = 56 * 1024 * 1024


def _cparams(n_axes):
    return pltpu.CompilerParams(dimension_semantics=("arbitrary",) * n_axes,
                                vmem_limit_bytes=VMEM_LIMIT)


def _sigmoid(x):
    return 0.5 + 0.5 * jnp.tanh(0.5 * x)


def _silu(x):
    h = 0.5 * x
    return h + h * jnp.tanh(h)


def _softplus(x):
    return jnp.maximum(x, 0.0) + jnp.log(1.0 + jnp.exp(-jnp.abs(x)))


def _layer_norm(v, g, b):
    mu = jnp.mean(v, axis=-1, keepdims=True)
    vc = v - mu
    var = jnp.mean(vc * vc, axis=-1, keepdims=True)
    return vc * lax.rsqrt(var + LN_EPS) * g + b


CAST_ROWS = 256


def _cast_weight(dst_ref, src_ref):
    def body(r, carry):
        rows = pl.ds(pl.multiple_of(r * CAST_ROWS, CAST_ROWS), CAST_ROWS)
        dst_ref[rows, :] = src_ref[rows, :].astype(BF16)
        return carry
    lax.fori_loop(0, src_ref.shape[0] // CAST_ROWS, body, 0)


def _mm_kernel(a_ref, w_ref, o_ref, wt_ref, *, act):
    @pl.when(pl.program_id(1) == 0)
    def _():
        _cast_weight(wt_ref, w_ref)

    acc = jnp.dot(a_ref[...], wt_ref[...], preferred_element_type=F32)
    if act == "relu2":
        r = jnp.maximum(acc, 0.0)
        acc = r * r
    o_ref[...] = acc.astype(o_ref.dtype)


def _matmul(a, w, layer, *, tm, tn, out_dtype, act=None):
    m, k = a.shape
    n = w.shape[2]
    return pl.pallas_call(
        functools.partial(_mm_kernel, act=act),
        grid=(n // tn, m // tm),
        in_specs=[pl.BlockSpec((tm, k), lambda j, i: (i, 0)),
                  pl.BlockSpec((None, k, tn), lambda j, i: (layer, 0, j))],
        out_specs=pl.BlockSpec((tm, tn), lambda j, i: (i, j)),
        out_shape=jax.ShapeDtypeStruct((m, n), out_dtype),
        scratch_shapes=[pltpu.VMEM((k, tn), BF16)],
        compiler_params=_cparams(2),
    )(a, w)


IN_SHIFT_QKVG = SSD_N_HEADS
IN_SHIFT_GATES = SSD_N_HEADS + GLA_GATE_RANK


def _in_proj_kernel(a_ref, w_ref, o_ref, wt_ref):
    @pl.when(pl.program_id(1) == 0)
    def _():
        _cast_weight(wt_ref, w_ref.at[0])

    o_ref[...] = lax.dot_general(a_ref[...], wt_ref[...], (((1,), (1,)), ((), ())),
                                 preferred_element_type=F32).astype(o_ref.dtype)


def _in_proj(a, w_in_t, layer, *, tm=2048, tn=1024):
    m, k = a.shape
    assert OFF_Q % tn == 0 and OFF_GS % tn == 0 and N_MAIN % tn == 0

    def w_rows(j, i):
        past_dt = (j >= OFF_Q // tn).astype(jnp.int32)
        past_gk = (j >= OFF_GS // tn).astype(jnp.int32)
        shift = past_dt * IN_SHIFT_QKVG + past_gk * (IN_SHIFT_GATES - IN_SHIFT_QKVG)
        return layer, pl.multiple_of(j * tn + shift, 16), 0

    return pl.pallas_call(
        _in_proj_kernel,
        grid=(N_MAIN // tn, m // tm),
        in_specs=[pl.BlockSpec((tm, k), lambda j, i: (i, 0)),
                  pl.BlockSpec((pl.Element(1), pl.Element(tn), pl.Element(k)), w_rows)],
        out_specs=pl.BlockSpec((tm, tn), lambda j, i: (i, j)),
        out_shape=jax.ShapeDtypeStruct((m, N_MAIN), BF16),
        scratch_shapes=[pltpu.VMEM((tn, k), BF16)],
        compiler_params=_cparams(2),
    )(a, w_in_t)


def _small_proj_kernel(a_ref, b_ref, o_ref, ot_ref):
    acc = lax.dot_general(a_ref[...], b_ref[...].astype(BF16), (((1,), (1,)), ((), ())),
                          preferred_element_type=F32)
    o_ref[...] = acc
    ot_ref[...] = acc.T


def _small_proj(a, b_t, layer, *, tm=1024):
    m, k = a.shape
    return pl.pallas_call(
        _small_proj_kernel,
        grid=(m // tm,),
        in_specs=[pl.BlockSpec((tm, k), lambda i: (i, 0)),
                  pl.BlockSpec((None, LANE, k), lambda i: (layer, 0, 0))],
        out_specs=[pl.BlockSpec((tm, LANE), lambda i: (i, 0)),
                   pl.BlockSpec((LANE, tm), lambda i: (0, i))],
        out_shape=[jax.ShapeDtypeStruct((m, LANE), F32),
                   jax.ShapeDtypeStruct((LANE, m), F32)],
        compiler_params=_cparams(1),
    )(a, b_t)


def _merge_kernel(y_ref, o_ref, ws_ref, wg_ref, gs_ref, gg_ref, gb_ref, out_ref, wst_ref, wgt_ref):
    @pl.when(pl.program_id(1) == 0)
    def _():
        _cast_weight(wst_ref, ws_ref)
        _cast_weight(wgt_ref, wg_ref)

    bs = jnp.dot(y_ref[...], wst_ref[...], preferred_element_type=F32)
    bg = jnp.dot(o_ref[...], wgt_ref[...], preferred_element_type=F32)
    gb = gb_ref[...]
    s_gate = _sigmoid(gs_ref[...].astype(F32) + gb[0:1, :])
    g_gate = _sigmoid(gg_ref[...].astype(F32) + gb[1:2, :])
    out_ref[...] = (s_gate * bs + g_gate * bg).astype(out_ref.dtype)


def _merge(y, o, proj, w_ssd, w_gla, gate_bias, layer, *, tm=1024, tn=512):
    t = y.shape[0]
    return pl.pallas_call(
        _merge_kernel,
        grid=(D_MODEL // tn, t // tm),
        in_specs=[pl.BlockSpec((tm, SSD_D_INNER), lambda j, i: (i, 0)),
                  pl.BlockSpec((tm, GLA_D_VALUE), lambda j, i: (i, 0)),
                  pl.BlockSpec((None, SSD_D_INNER, tn), lambda j, i: (layer, 0, j),
                               pipeline_mode=pl.Buffered(1)),
                  pl.BlockSpec((None, GLA_D_VALUE, tn), lambda j, i: (layer, 0, j),
                               pipeline_mode=pl.Buffered(1)),
                  pl.BlockSpec((tm, tn), lambda j, i: (i, OFF_GS // tn + j)),
                  pl.BlockSpec((tm, tn), lambda j, i: (i, OFF_GG // tn + j)),
                  pl.BlockSpec((None, 2, tn), lambda j, i: (layer, 0, j))],
        out_specs=pl.BlockSpec((tm, tn), lambda j, i: (i, j)),
        out_shape=jax.ShapeDtypeStruct((t, D_MODEL), BF16),
        scratch_shapes=[pltpu.VMEM((SSD_D_INNER, tn), BF16), pltpu.VMEM((GLA_D_VALUE, tn), BF16)],
        compiler_params=_cparams(2),
    )(y, o, w_ssd, w_gla, proj, proj, gate_bias)


def _proj_ln_kernel(a_ref, w_ref, res_ref, g_ref, b_ref, of_ref, ob_ref):
    acc = jnp.dot(a_ref[...], w_ref[...], preferred_element_type=F32)
    out = _layer_norm(ALPHA * res_ref[...] + acc, g_ref[...], b_ref[...])
    of_ref[...] = out
    ob_ref[...] = out.astype(BF16)


def _proj_ln(a, w, res, g, b, layer, *, tm=512):
    t, k = a.shape
    return pl.pallas_call(
        _proj_ln_kernel,
        grid=(t // tm,),
        in_specs=[pl.BlockSpec((tm, k), lambda i: (i, 0)),
                  pl.BlockSpec((None, k, D_MODEL), lambda i: (layer, 0, 0)),
                  pl.BlockSpec((tm, D_MODEL), lambda i: (i, 0)),
                  pl.BlockSpec((None, 1, D_MODEL), lambda i: (layer, 0, 0)),
                  pl.BlockSpec((None, 1, D_MODEL), lambda i: (layer, 0, 0))],
        out_specs=[pl.BlockSpec((tm, D_MODEL), lambda i: (i, 0)),
                   pl.BlockSpec((tm, D_MODEL), lambda i: (i, 0))],
        out_shape=[jax.ShapeDtypeStruct((t, D_MODEL), F32),
                   jax.ShapeDtypeStruct((t, D_MODEL), BF16)],
        compiler_params=_cparams(1),
    )(a, w, res, g, b)


def _down_ln_kernel(u_ref, w_ref, res_ref, g_ref, b_ref, of_ref, ob_ref, acc_ref):
    kk = pl.program_id(1)

    @pl.when(kk == 0)
    def _():
        acc_ref[...] = jnp.zeros_like(acc_ref)

    acc_ref[...] += jnp.dot(u_ref[...], w_ref[...], preferred_element_type=F32)

    @pl.when(kk == pl.num_programs(1) - 1)
    def _():
        out = _layer_norm(ALPHA * res_ref[...] + acc_ref[...], g_ref[...], b_ref[...])
        of_ref[...] = out
        ob_ref[...] = out.astype(BF16)


def _down_ln(u, w, res, g, b, layer, *, tm=512, tk=2048):
    t, k = u.shape
    return pl.pallas_call(
        _down_ln_kernel,
        grid=(t // tm, k // tk),
        in_specs=[pl.BlockSpec((tm, tk), lambda i, kk: (i, kk)),
                  pl.BlockSpec((None, tk, D_MODEL), lambda i, kk: (layer, kk, 0)),
                  pl.BlockSpec((tm, D_MODEL), lambda i, kk: (i, 0)),
                  pl.BlockSpec((None, 1, D_MODEL), lambda i, kk: (layer, 0, 0)),
                  pl.BlockSpec((None, 1, D_MODEL), lambda i, kk: (layer, 0, 0))],
        out_specs=[pl.BlockSpec((tm, D_MODEL), lambda i, kk: (i, 0)),
                   pl.BlockSpec((tm, D_MODEL), lambda i, kk: (i, 0))],
        out_shape=[jax.ShapeDtypeStruct((t, D_MODEL), F32),
                   jax.ShapeDtypeStruct((t, D_MODEL), BF16)],
        scratch_shapes=[pltpu.VMEM((tm, D_MODEL), F32)],
        compiler_params=_cparams(2),
    )(u, w, res, g, b)


_EXP_SKIP = 1
_EXP_SECTIONS = 4


def _ssd_expand_matrix():
    r = lax.broadcasted_iota(jnp.int32, (LANE, _EXP_SECTIONS * SSD_GROUP_W), 0)
    c = lax.broadcasted_iota(jnp.int32, (LANE, _EXP_SECTIONS * SSD_GROUP_W), 1)
    src = (c // SSD_GROUP_W + _EXP_SKIP) * SSD_HEADS_PER_GROUP + (c % SSD_GROUP_W) // SSD_HEAD_DIM
    e = (r == src).astype(BF16)
    return jnp.concatenate([e, e], axis=0)


def _ssd_shift_matrix():
    q = SSD_CHUNK
    r = lax.broadcasted_iota(jnp.int32, (3 * q, 2 * q), 0)
    c = lax.broadcasted_iota(jnp.int32, (3 * q, 2 * q), 1)
    return (c == q + r % q - (r // q + 1)).astype(BF16)


def _split3(v):
    hi = v.astype(BF16)
    r1 = v - hi.astype(F32)
    mid = r1.astype(BF16)
    lo = (r1 - mid.astype(F32)).astype(BF16)
    return hi, mid, lo


def _ssd_kernel(x_ref, b_ref, c_ref, z_ref, dt_ref, cw_ref, cb_ref, dtb_ref, alog_ref,
                dexp_ref, nw_ref, e_ref, shift_ref, y_ref, tail_ref, st_ref, *, chunks_per_step):
    q = SSD_CHUNK
    hg = SSD_HEADS_PER_GROUP
    p = SSD_HEAD_DIM
    gw = SSD_GROUP_W
    n = SSD_D_STATE

    @pl.when(pl.program_id(2) == 0)
    def _():
        tail_ref[...] = jnp.zeros_like(tail_ref)
        st_ref[...] = jnp.zeros_like(st_ref)

    row = lax.broadcasted_iota(jnp.int32, (q, q), 0)
    col = lax.broadcasted_iota(jnp.int32, (q, q), 1)
    causal = col <= row
    tri_b = causal.astype(BF16)
    even_lanes = col < p
    nt = (((1,), (1,)), ((), ()))

    cw = cw_ref[...]
    cbias = cb_ref[...]
    dtb = dtb_ref[...]
    a_neg = -jnp.exp(alog_ref[...])
    dexp = dexp_ref[...]
    nw = nw_ref[...]
    shift = shift_ref[...]

    def dot(a, b, dims=None):
        if dims is None:
            return jnp.dot(a, b, preferred_element_type=F32)
        return lax.dot_general(a, b, dims, preferred_element_type=F32)

    cs_ = range(chunks_per_step)
    rows = [pl.ds(cc * q, q) for cc in cs_]

    raw = [jnp.concatenate([x_ref[r, :], b_ref[r, :], c_ref[r, :]], axis=1) for r in rows]
    prev = [tail_ref[...]] + raw[:-1]
    sh = [dot(shift, jnp.concatenate([prev[cc], raw[cc]], axis=0)) for cc in cs_]
    xbc = []
    for cc in cs_:
        acc = cbias + cw[3:4, :] * raw[cc].astype(F32)
        for s in range(1, SSD_CONV_W):
            acc = acc + cw[3 - s:4 - s, :] * sh[cc][(s - 1) * q:s * q, :]
        xbc.append(_silu(acc))
    xs = [v[:, 0:gw] for v in xbc]
    bm_b = [v[:, gw:gw + n].astype(BF16) for v in xbc]
    cm_b = [v[:, gw + n:].astype(BF16) for v in xbc]

    dt_t = [_softplus(dt_ref[:, cc * q:(cc + 1) * q] + dtb) for cc in cs_]
    a_cs_t = [sum(dot(part, tri_b, nt) for part in _split3(v * a_neg))
              for v in dt_t]
    cs = []
    for cc in cs_:
        a_last = a_cs_t[cc][:, q - 1:q]
        stack = jnp.concatenate(
            [a_cs_t[cc], dt_t[cc], jnp.exp(a_cs_t[cc]), jnp.exp(a_last - a_cs_t[cc]),
             jnp.broadcast_to(jnp.exp(a_last), (hg, q)),
             jnp.zeros((q - (_EXP_SECTIONS + _EXP_SKIP) * hg, q), F32)], axis=0)
        cs.append(stack.T)
    hilo = []
    for v in cs:
        hi = v.astype(BF16)
        hilo.append(jnp.concatenate([hi, (v - hi.astype(F32)).astype(BF16)], axis=1))
    ex = [dot(v, e_ref[:, 0:3 * gw]) for v in hilo]
    el_exp = [dot(v[0:hg, :], e_ref[:, 3 * gw:4 * gw])[0:1, :] for v in hilo]

    xdt = [xs[cc] * ex[cc][:, 0:gw] for cc in cs_]
    xdt_b = [v.astype(BF16) for v in xdt]
    xw_b = [(xdt[cc] * ex[cc][:, 2 * gw:3 * gw]).astype(BF16) for cc in cs_]
    cbm = [jnp.where(causal, dot(cm_b[cc], bm_b[cc], nt), 0.0) for cc in cs_]
    states = [dot(bm_b[cc], xw_b[cc], (((0,), (0,)), ((), ()))) for cc in cs_]

    y_diag = []
    for cc in cs_:
        ys = []
        for hp in range(hg // 2):
            l_mats = []
            for hh in (2 * hp, 2 * hp + 1):
                seg = cs[cc][:, hh:hh + 1] - a_cs_t[cc][hh:hh + 1, :]
                l_mats.append((cbm[cc] * jnp.exp(jnp.minimum(seg, 0.0))).astype(BF16))
            x_pair = xdt_b[cc][:, 2 * hp * p:(2 * hp + 2) * p]
            zero = jnp.zeros_like(x_pair)
            rhs = jnp.concatenate([jnp.where(even_lanes, x_pair, zero),
                                   jnp.where(even_lanes, zero, x_pair)], axis=0)
            ys.append(dot(jnp.concatenate(l_mats, axis=1), rhs))
        y_diag.append(jnp.concatenate(ys, axis=1))

    st = st_ref[...]
    y_off = []
    for cc in cs_:
        y_off.append(dot(cm_b[cc], st.astype(BF16)) * ex[cc][:, gw:2 * gw])
        st = st * el_exp[cc] + states[cc]
    st_ref[...] = st
    tail_ref[...] = raw[-1]

    for cc in cs_:
        y = y_diag[cc] + y_off[cc] + dexp * xs[cc]
        y = y * _silu(z_ref[rows[cc], :].astype(F32))
        y = y * lax.rsqrt(jnp.mean(y * y, axis=-1, keepdims=True) + RMS_EPS) * nw
        y_ref[rows[cc], :] = y.astype(y_ref.dtype)


def _ssd(proj, small_t, cw_g, cb_g, dtb_g, alog_g, dexp_g, nw_g, *, batch, seq, chunks_per_step=16):
    rows = SSD_CHUNK * chunks_per_step
    steps = seq // rows
    gw = SSD_GROUP_W
    n = SSD_D_STATE
    hg = SSD_HEADS_PER_GROUP
    e_mat = _ssd_expand_matrix()
    shift = _ssd_shift_matrix()

    def row_blk(b, c):
        return b * steps + c

    return pl.pallas_call(
        functools.partial(_ssd_kernel, chunks_per_step=chunks_per_step),
        grid=(batch, SSD_N_GROUPS, steps),
        in_specs=[
            pl.BlockSpec((rows, gw), lambda b, g, c: (row_blk(b, c), OFF_X // gw + g)),
            pl.BlockSpec((rows, n), lambda b, g, c: (row_blk(b, c), OFF_B // n + g)),
            pl.BlockSpec((rows, n), lambda b, g, c: (row_blk(b, c), OFF_C // n + g)),
            pl.BlockSpec((rows, gw), lambda b, g, c: (row_blk(b, c), OFF_Z // gw + g)),
            pl.BlockSpec((hg, rows), lambda b, g, c: (g, row_blk(b, c))),
            pl.BlockSpec((None, SSD_CONV_W, SSD_GROUP_CONV_W), lambda b, g, c: (g, 0, 0)),
            pl.BlockSpec((None, 1, SSD_GROUP_CONV_W), lambda b, g, c: (g, 0, 0)),
            pl.BlockSpec((None, hg, LANE), lambda b, g, c: (g, 0, 0)),
            pl.BlockSpec((None, hg, LANE), lambda b, g, c: (g, 0, 0)),
            pl.BlockSpec((None, 1, gw), lambda b, g, c: (g, 0, 0)),
            pl.BlockSpec((None, 1, gw), lambda b, g, c: (g, 0, 0)),
            pl.BlockSpec(e_mat.shape, lambda b, g, c: (0, 0)),
            pl.BlockSpec(shift.shape, lambda b, g, c: (0, 0)),
        ],
        out_specs=pl.BlockSpec((rows, gw), lambda b, g, c: (row_blk(b, c), g)),
        out_shape=jax.ShapeDtypeStruct((batch * seq, SSD_D_INNER), BF16),
        scratch_shapes=[pltpu.VMEM((SSD_CHUNK, SSD_GROUP_CONV_W), BF16),
                        pltpu.VMEM((SSD_D_STATE, gw), F32)],
        compiler_params=_cparams(3),
    )(proj, proj, proj, proj, small_t, cw_g, cb_g, dtb_g, alog_g, dexp_g, nw_g, e_mat, shift)


def _gla_cumsum_matrix(rows):
    r = lax.broadcasted_iota(jnp.int32, (rows, rows), 0)
    c = lax.broadcasted_iota(jnp.int32, (rows, rows), 1)
    return jnp.logical_and(r // GLA_CHUNK == c // GLA_CHUNK, c <= r).astype(F32)


def _gla_kernel(q_ref, k_ref, v_ref, g_ref, sm_ref, gkw_ref, gkb_ref, nw_ref, tri_ref, trib_ref,
                o_ref, st_ref, *, chunks_per_step):
    @pl.when(pl.program_id(1) == 0)
    def _():
        st_ref[...] = jnp.zeros_like(st_ref)

    c = GLA_CHUNK
    hk, hv = GLA_HEAD_K, GLA_HEAD_V
    hps = GLA_HEADS_PER_STEP
    scale = hk ** -0.5
    nt = (((1,), (1,)), ((), ()))
    tn = (((0,), (0,)), ((), ()))
    bs = range(q_ref.shape[0])
    chains = [(b, h) for b in bs for h in range(hps)]
    ch = range(len(chains))

    def dot(a, b, dims=None):
        if dims is None:
            return jnp.dot(a, b, preferred_element_type=F32)
        return lax.dot_general(a, b, dims, preferred_element_type=F32)

    def kcols(h):
        return slice(h * hk, (h + 1) * hk)

    def vcols(h):
        return slice(h * hv, (h + 1) * hv)

    tri = tri_ref[...]
    tri_b = trib_ref[...]
    nw = nw_ref[...]
    w = [_split3(gkw_ref[:, kcols(h)]) for h in range(hps)]
    sm = [_split3(sm_ref[b]) for b in bs]
    pre = [dot(sm[b][0], w[h][0]) + dot(sm[b][0], w[h][1]) + dot(sm[b][1], w[h][0])
           + gkb_ref[:, kcols(h)] for b, h in chains]
    gk = [-_softplus(-p) * (1.0 / GLA_GATE_NORMALIZER) for p in pre]
    gcs = [sum(dot(tri_b, part) for part in _split3(g)) for g in gk]
    kf = [k_ref[b, :, kcols(h)].astype(F32) for b, h in chains]
    vb = [v_ref[b, :, vcols(h)] for b, h in chains]
    qg = [(q_ref[b, :, kcols(h)].astype(F32) * scale * jnp.exp(gcs[i])).astype(BF16)
          for i, (b, h) in enumerate(chains)]
    kg = [(kf[i] * jnp.exp(-gcs[i])).astype(BF16) for i in ch]
    att = [(dot(qg[i], kg[i], nt) * tri).astype(BF16) for i in ch]
    o_intra = [dot(att[i], vb[i]) for i in ch]

    st = [st_ref[i] for i in ch]
    outs = [[] for _ in ch]
    for cc in range(chunks_per_step):
        rs = slice(cc * c, (cc + 1) * c)
        for i in ch:
            g_last = gcs[i][(cc + 1) * c - 1:(cc + 1) * c, :]
            kd = (kf[i][rs] * jnp.exp(g_last - gcs[i][rs])).astype(BF16)
            outs[i].append(o_intra[i][rs] + dot(qg[i][rs], st[i].astype(BF16), nt))
            st[i] = st[i] * jnp.exp(g_last) + dot(vb[i][rs], kd, tn)
    for i in ch:
        st_ref[i] = st[i]
    for i, (b, h) in enumerate(chains):
        o = jnp.concatenate(outs[i], axis=0)
        o = o * lax.rsqrt(jnp.mean(o * o, axis=-1, keepdims=True) + RMS_EPS) * nw
        o_ref[b, :, vcols(h)] = (o * _silu(g_ref[b, :, vcols(h)].astype(F32))).astype(o_ref.dtype)


def _gla(proj, small, gkw_pad, gkb, nw, *, batch, seq, chunks_per_step=4):
    rows = GLA_CHUNK * chunks_per_step
    steps = seq // rows
    hps = GLA_HEADS_PER_STEP
    wk, wv = hps * GLA_HEAD_K, hps * GLA_HEAD_V
    tri = _gla_cumsum_matrix(rows)

    proj3 = proj.reshape(batch, seq, N_MAIN)
    small3 = small.reshape(batch, seq, LANE)
    out = pl.pallas_call(
        functools.partial(_gla_kernel, chunks_per_step=chunks_per_step),
        grid=(GLA_N_HEADS // hps, steps),
        in_specs=[
            pl.BlockSpec((batch, rows, wk), lambda h, c: (0, c, OFF_Q // wk + h)),
            pl.BlockSpec((batch, rows, wk), lambda h, c: (0, c, OFF_K // wk + h)),
            pl.BlockSpec((batch, rows, wv), lambda h, c: (0, c, OFF_V // wv + h)),
            pl.BlockSpec((batch, rows, wv), lambda h, c: (0, c, OFF_G // wv + h)),
            pl.BlockSpec((batch, rows, LANE), lambda h, c: (0, c, 0)),
            pl.BlockSpec((LANE, wk), lambda h, c: (0, h)),
            pl.BlockSpec((1, wk), lambda h, c: (0, h)),
            pl.BlockSpec((1, GLA_HEAD_V), lambda h, c: (0, 0)),
            pl.BlockSpec((rows, rows), lambda h, c: (0, 0)),
            pl.BlockSpec((rows, rows), lambda h, c: (0, 0)),
        ],
        out_specs=pl.BlockSpec((batch, rows, wv), lambda h, c: (0, c, h)),
        out_shape=jax.ShapeDtypeStruct((batch, seq, GLA_D_VALUE), BF16),
        scratch_shapes=[pltpu.VMEM((batch * hps, GLA_HEAD_V, GLA_HEAD_K), F32)],
        compiler_params=_cparams(2),
    )(proj3, proj3, proj3, proj3, small3, gkw_pad, gkb, nw, tri, tri.astype(BF16))
    return out.reshape(batch * seq, GLA_D_VALUE)


def _small_in_proj_weight(w_in_t):
    c0 = OFF_Q
    c1 = c0 + SSD_N_HEADS
    c2 = c1 + 2 * GLA_D_KEY + 2 * GLA_D_VALUE
    c3 = c2 + GLA_GATE_RANK
    layers, _, k = w_in_t.shape
    pad = jnp.zeros((layers, LANE - SSD_N_HEADS - GLA_GATE_RANK, k), w_in_t.dtype)
    return jnp.concatenate([w_in_t[:, c0:c1, :], w_in_t[:, c2:c3, :], pad], axis=1)


def _head_rows(v):
    vg = v.reshape(SSD_N_GROUPS, SSD_HEADS_PER_GROUP, 1)
    return jnp.broadcast_to(vg, (SSD_N_GROUPS, SSD_HEADS_PER_GROUP, LANE))


def _layer(layer, xf, xb, batch, seq, w_in, w_small, conv_w, conv_b, dt_bias, a_log, d_skip, ssd_norm_w,
           gk_w, gk_b, gla_norm_w, w_ssd, w_gla, gate_bias, w_out_b, ln1_g, ln1_b,
           w_up, w_down_b, ln2_g, ln2_b):
    proj = _in_proj(xb, w_in, layer)
    small, small_t = _small_proj(xb, w_small, layer)

    gw, n = SSD_GROUP_W, SSD_D_STATE
    ng = SSD_N_GROUPS

    def conv_cols(v):
        lead = v.shape[:-1]
        xs = v[..., :SSD_D_INNER].reshape(lead + (ng, gw))
        bs = v[..., SSD_D_INNER:SSD_D_INNER + ng * n].reshape(lead + (ng, n))
        cs = v[..., SSD_D_INNER + ng * n:].reshape(lead + (ng, n))
        return jnp.moveaxis(jnp.concatenate([xs, bs, cs], axis=-1), -2, 0)

    cw_g = conv_cols(conv_w)
    cb_g = conv_cols(conv_b[None, :])
    dtb_g = _head_rows(dt_bias)
    alog_g = _head_rows(a_log)
    dexp_g = jnp.repeat(d_skip, SSD_HEAD_DIM).reshape(ng, 1, gw)
    nw_g = ssd_norm_w.reshape(ng, 1, gw)
    y = _ssd(proj, small_t, cw_g, cb_g, dtb_g, alog_g, dexp_g, nw_g, batch=batch, seq=seq)

    gkw_pad = jnp.zeros((LANE, GLA_D_KEY), F32).at[SMALL_GK:SMALL_GK + GLA_GATE_RANK].set(gk_w)
    o = _gla(proj, small, gkw_pad, gk_b[None, :], gla_norm_w[None, :], batch=batch, seq=seq)

    merged = _merge(y, o, proj, w_ssd, w_gla, gate_bias, layer)
    hf, hb = _proj_ln(merged, w_out_b, xf, ln1_g, ln1_b, layer)
    u = _matmul(hb, w_up, layer, tm=2048, tn=1024, out_dtype=BF16, act="relu2")
    return _down_ln(u, w_down_b, hf, ln2_g, ln2_b, layer)


def kernel(x, w_in, ssd_conv_w, ssd_conv_b, ssd_dt_bias, ssd_A_log, ssd_D, ssd_norm_w, gla_gk_w, gla_gk_b, gla_norm_w, w_ssd_branch, w_gla_branch, gate_bias, w_out, ln1_g, ln1_b, w_up, w_down, ln2_g, ln2_b):
    batch, seq, d = x.shape
    xf = x.reshape(batch * seq, d)
    xb = xf.astype(BF16)
    w_out_b = w_out.astype(BF16)
    w_down_b = w_down.astype(BF16)
    ln1_g, ln1_b, ln2_g, ln2_b = (v[:, None, :] for v in (ln1_g, ln1_b, ln2_g, ln2_b))
    w_in_t = jnp.swapaxes(w_in, 1, 2)
    w_small = _small_in_proj_weight(w_in_t)
    for l in range(w_in.shape[0]):
        xf, xb = _layer(l, xf, xb, batch, seq, w_in_t, w_small, ssd_conv_w[l], ssd_conv_b[l], ssd_dt_bias[l],
                        ssd_A_log[l], ssd_D[l], ssd_norm_w[l], gla_gk_w[l], gla_gk_b[l],
                        gla_norm_w[l], w_ssd_branch, w_gla_branch, gate_bias, w_out_b,
                        ln1_g, ln1_b, w_up, w_down_b, ln2_g, ln2_b)
    return xf.reshape(batch, seq, d)
```
